```python
import jax
import jax.numpy as jnp
from jax import lax
import numpy as np

D_MODEL = 1024
BATCH = 32
SEQ = 2048
DEPTH = 4

N_MIXERS = 2
HEAD_SIZE = 64
N_HEADS = D_MODEL // HEAD_SIZE
DECAY_LORA = 64
ICLR_LORA = 64
VALUE_LORA = 32
GATE_LORA = 128
GN_EPS = HEAD_SIZE * 1e-5
CONV_WIDTH = 3
N_EXPERTS = 32
TOP_K = 4
D_FF = D_MODEL
SWIGLU_LIMIT = 7.0
SWIGLU_ALPHA = 1.702
BLOCK_ROWS = 256
PLE_DIM = 256
LN_EPS = 1e-5
DEEPNORM_ALPHA = (2.0 * DEPTH) ** 0.25
DEEPNORM_BETA = (8.0 * DEPTH) ** -0.25
N_RWKV = (DEPTH + N_MIXERS - 1) // N_MIXERS
N_CONV = DEPTH // N_MIXERS

kernel_name = 'rwkv7_shortconv_moe_deepnorm_hybrid'


def layer_norm(x, g, b):
    xf = x.astype(jnp.float32)
    mu = jnp.mean(xf, axis=-1, keepdims=True)
    var = jnp.mean(jnp.square(xf - mu), axis=-1, keepdims=True)
    return ((xf - mu) * lax.rsqrt(var + LN_EPS) * g + b).astype(x.dtype)


def token_shift(x):
    return jnp.pad(x, ((0, 0), (1, 0), (0, 0)))[:, :-1]


def wkv7_scan(r, decay, k, v, a, b):
    def step(S, inp):
        r_t, w_t, k_t, v_t, a_t, b_t = inp
        sa = jnp.einsum('bhij,bhj->bhi', S, a_t)
        S = S * w_t[:, :, None, :] + sa[..., None] * b_t[:, :, None, :] + v_t[..., None] * k_t[:, :, None, :]
        return S, jnp.einsum('bhij,bhj->bhi', S, r_t)
    seq = tuple(jnp.moveaxis(z.astype(jnp.float32), 1, 0) for z in (r, decay, k, v, a, b))
    bsz, _, h, n = r.shape
    s0 = jnp.zeros((bsz, h, n, n), jnp.float32)
    _, y = lax.scan(step, s0, seq)
    return jnp.moveaxis(y, 0, 1)


def rwkv7_time_mix(x, v_first, mix, w_rkv, w0, w1, w2, a0, a1, a2, g1, g2,
                   k_k, k_a, r_k, lnx_g, lnx_b, w_o, v_lora):
    bsz, t, d = x.shape
    xx = token_shift(x) - x
    x_rkv = x + xx * mix[:3, None, None, :]
    r, k, v = jnp.einsum('cbtd,cde->cbte', x_rkv, w_rkv)
    xw = x + xx * mix[3]
    xa = x + xx * mix[4]
    xg = x + xx * mix[5]
    w = -jax.nn.softplus(-(w0 + jnp.tanh(xw @ w1) @ w2).astype(jnp.float32)) - 0.5
    a = jax.nn.sigmoid(a0 + (xa @ a1) @ a2)
    g = jax.nn.sigmoid(xg @ g1) @ g2
    if v_lora is None:
        v_first = v
    else:
        v0, v1, v2 = v_lora
        v = v + (v_first - v) * jax.nn.sigmoid(v0 + (x_rkv[2] @ v1) @ v2)

    def heads(z):
        return z.reshape(bsz, t, N_HEADS, HEAD_SIZE)

    kk = heads(k * k_k).astype(jnp.float32)
    kk = kk / jnp.maximum(jnp.sqrt(jnp.sum(kk * kk, axis=-1, keepdims=True)), 1e-12)
    k = k * (1 + (a - 1) * k_a)
    r, k, v, a = heads(r), heads(k), heads(v), heads(a).astype(jnp.float32)
    decay = jnp.exp(-jnp.exp(heads(w)))
    y = wkv7_scan(r, decay, k, v, -kk, kk * a)
    mu = jnp.mean(y, axis=-1, keepdims=True)
    var = jnp.mean(jnp.square(y - mu), axis=-1, keepdims=True)
    y = ((y - mu) * lax.rsqrt(var + GN_EPS)).reshape(bsz, t, d) * lnx_g + lnx_b
    bonus = jnp.sum(r.astype(jnp.float32) * k.astype(jnp.float32) * r_k, axis=-1, keepdims=True) * v.astype(jnp.float32)
    y = (y + bonus.reshape(bsz, t, d)).astype(x.dtype)
    return (y * g) @ w_o, v_first


def short_conv_mix(x, w_in, conv_w, w_out):
    d = x.shape[-1]
    gate_b, gate_c, h = jnp.split(x @ w_in, 3, axis=-1)
    u = lax.conv_general_dilated(gate_c * h, conv_w[:, None, :], window_strides=(1,),
                                 padding=[(CONV_WIDTH - 1, 0)],
                                 dimension_numbers=('NWC', 'WIO', 'NWC'),
                                 feature_group_count=d)
    return (gate_b * u) @ w_out


def moe_ffn(x2, router_w, router_b, w_gu, b_gu, w_down, b_down):
    n, d = x2.shape
    logits = x2.astype(jnp.float32) @ router_w.astype(jnp.float32) + router_b.astype(jnp.float32)
    top_val, top_idx = lax.top_k(logits, TOP_K)
    gate = jax.nn.softmax(top_val, axis=-1)
    n_assign = n * TOP_K
    flat_e = top_idx.reshape(n_assign)
    order = jnp.argsort(flat_e)
    sorted_e = flat_e[order]
    sorted_tok = order // TOP_K
    sorted_gate = gate.reshape(n_assign)[order]
    counts = jnp.bincount(flat_e, length=N_EXPERTS)
    padded = (counts + BLOCK_ROWS - 1) // BLOCK_ROWS * BLOCK_ROWS
    pad_end = jnp.cumsum(padded)
    pad_start = pad_end - padded
    start = jnp.cumsum(counts) - counts
    dest = pad_start[sorted_e] + jnp.arange(n_assign, dtype=jnp.int32) - start[sorted_e]
    n_blocks = -(-n_assign // BLOCK_ROWS) + N_EXPERTS
    rows = n_blocks * BLOCK_ROWS
    buf = jnp.zeros((rows, d), x2.dtype).at[dest].set(x2[sorted_tok])
    block_e = jnp.minimum(jnp.searchsorted(pad_end, jnp.arange(n_blocks) * BLOCK_ROWS, side='right'),
                          N_EXPERTS - 1)

    def expert_block(args):
        xb, e = args
        hcat = xb @ w_gu[e] + b_gu[e]
        glu = jnp.minimum(hcat[:, :D_FF], SWIGLU_LIMIT)
        lin = jnp.clip(hcat[:, D_FF:], -SWIGLU_LIMIT, SWIGLU_LIMIT)
        act = glu * jax.nn.sigmoid(SWIGLU_ALPHA * glu) * (lin + 1)
        return act @ w_down[e] + b_down[e]

    out = lax.map(expert_block, (buf.reshape(n_blocks, BLOCK_ROWS, d), block_e)).reshape(rows, d)
    contrib = out[dest] * sorted_gate[:, None].astype(out.dtype)
    return jax.ops.segment_sum(contrib, sorted_tok, num_segments=n)


def setup_inputs(seed: int = 0) -> dict:
    key = jax.random.key(seed)
    ks = iter(jax.random.split(key, 48))

    def nrm(shape, scale):
        return scale * jax.random.normal(next(ks), shape, jnp.float32)

    def uni(shape, lo, hi):
        return jax.random.uniform(next(ks), shape, jnp.float32, lo, hi)

    D, E, F, NR, NC, L = D_MODEL, N_EXPERTS, D_FF, N_RWKV, N_CONV, DEPTH
    fan = D ** -0.5
    beta = DEEPNORM_BETA
    return {
        'x': nrm((BATCH, SEQ, D), 1.0),
        'p': nrm((L, BATCH, SEQ, PLE_DIM), 1.0),
        'rwkv_mix': uni((NR, 6, D), 0.0, 1.0),
        'rwkv_w_rkv': nrm((NR, 3, D, D), fan) * jnp.array([1.0, 1.0, beta], jnp.float32)[None, :, None, None],
        'rwkv_w0': uni((NR, D), -6.0, 1.0),
        'rwkv_w1': nrm((NR, D, DECAY_LORA), fan),
        'rwkv_w2': nrm((NR, DECAY_LORA, D), 0.5 * DECAY_LORA ** -0.5),
        'rwkv_a0': nrm((NR, D), 0.1),
        'rwkv_a1': nrm((NR, D, ICLR_LORA), fan),
        'rwkv_a2': nrm((NR, ICLR_LORA, D), 0.5 * ICLR_LORA ** -0.5),
        'rwkv_v0': 1.0 + nrm((NR - 1, D), 0.1),
        'rwkv_v1': nrm((NR - 1, D, VALUE_LORA), fan),
        'rwkv_v2': nrm((NR - 1, VALUE_LORA, D), 0.5 * VALUE_LORA ** -0.5),
        'rwkv_g1': nrm((NR, D, GATE_LORA), fan),
        'rwkv_g2': nrm((NR, GATE_LORA, D), GATE_LORA ** -0.5),
        'rwkv_k_k': 0.85 + nrm((NR, D), 0.05),
        'rwkv_k_a': 1.0 + nrm((NR, D), 0.05),
        'rwkv_r_k': nrm((NR, N_HEADS, HEAD_SIZE), 0.1),
        'rwkv_lnx_g': 1.0 + nrm((NR, D), 0.05),
        'rwkv_lnx_b': nrm((NR, D), 0.02),
        'rwkv_w_o': nrm((NR, D, D), fan * beta),
        'conv_w_in': nrm((NC, D, 3 * D), fan),
        'conv_w': nrm((NC, CONV_WIDTH, D), CONV_WIDTH ** -0.5),
        'conv_w_out': nrm((NC, D, D), fan * beta),
        'ln_mix_g': 1.0 + nrm((L, D), 0.05),
        'ln_mix_b': nrm((L, D), 0.02),
        'router_w': nrm((L, D, E), fan),
        'router_b': nrm((L, E), 0.01),
        'moe_w_gu': nrm((L, E, D, 2 * F), fan),
        'moe_b_gu': nrm((L, E, 2 * F), 0.01),
        'moe_w_down': nrm((L, E, F, D), F ** -0.5 * beta),
        'moe_b_down': nrm((L, E, D), 0.01),
        'ln_ffn_g': 1.0 + nrm((L, D), 0.05),
        'ln_ffn_b': nrm((L, D), 0.02),
        'ple_w_proj': nrm((L, PLE_DIM, D), PLE_DIM ** -0.5 * beta),
        'ple_w_gate': nrm((L, D, D), fan),
        'ple_b_gate': nrm((L, D), 0.01),
        'ln_ple_g': 1.0 + nrm((L, D), 0.05),
        'ln_ple_b': nrm((L, D), 0.02),
    }


def reference(x, p, rwkv_mix, rwkv_w_rkv, rwkv_w0, rwkv_w1, rwkv_w2, rwkv_a0, rwkv_a1, rwkv_a2,
              rwkv_v0, rwkv_v1, rwkv_v2, rwkv_g1, rwkv_g2, rwkv_k_k, rwkv_k_a, rwkv_r_k,
              rwkv_lnx_g, rwkv_lnx_b, rwkv_w_o, conv_w_in, conv_w, conv_w_out,
              ln_mix_g, ln_mix_b, router_w, router_b, moe_w_gu, moe_b_gu, moe_w_down, moe_b_down,
              ln_ffn_g, ln_ffn_b, ple_w_proj, ple_w_gate, ple_b_gate, ln_ple_g, ln_ple_b):
    bsz, t, d = x.shape
    v_first = None
    for i in range(DEPTH):
        j = i // N_MIXERS
        if i % N_MIXERS == 0:
            v_lora = None if j == 0 else (rwkv_v0[j - 1], rwkv_v1[j - 1], rwkv_v2[j - 1])
            mixed, v_first = rwkv7_time_mix(
                x, v_first, rwkv_mix[j], rwkv_w_rkv[j], rwkv_w0[j], rwkv_w1[j], rwkv_w2[j],
                rwkv_a0[j], rwkv_a1[j], rwkv_a2[j], rwkv_g1[j], rwkv_g2[j], rwkv_k_k[j], rwkv_k_a[j],
                rwkv_r_k[j], rwkv_lnx_g[j], rwkv_lnx_b[j], rwkv_w_o[j], v_lora)
        else:
            mixed = short_conv_mix(x, conv_w_in[j], conv_w[j], conv_w_out[j])
        x = layer_norm(DEEPNORM_ALPHA * x + mixed, ln_mix_g[i], ln_mix_b[i])
        ffn = moe_ffn(x.reshape(bsz * t, d), router_w[i], router_b[i], moe_w_gu[i], moe_b_gu[i],
                      moe_w_down[i], moe_b_down[i]).reshape(bsz, t, d)
        x = layer_norm(DEEPNORM_ALPHA * x + ffn, ln_ffn_g[i], ln_ffn_b[i])
        ple = (p[i] @ ple_w_proj[i]) * jax.nn.sigmoid(x @ ple_w_gate[i] + ple_b_gate[i])
        x = layer_norm(DEEPNORM_ALPHA * x + ple, ln_ple_g[i], ln_ple_b[i])
    return x
```

```python
import functools

import jax
import jax.numpy as jnp
from jax import lax
from jax.experimental import pallas as pl
from jax.experimental.pallas import tpu as pltpu

DEPTH = 4
HEAD_SIZE = 64
LANES = 128
N_EXPERTS = 32
TOP_K = 4
GN_EPS = HEAD_SIZE * 1e-5
LN_EPS = 1e-5
SWIGLU_LIMIT = 7.0
SWIGLU_ALPHA = 1.702
DEEPNORM_ALPHA = (2.0 * DEPTH) ** 0.25
WKV_CHUNK = 64
VMEM_LIMIT = 56 * 1024 * 1024

F32 = jnp.float32
BF16 = jnp.bfloat16


def _dot(a, b):
    return jnp.dot(a, b, preferred_element_type=F32)


def _dot_nt(a, b):
    return lax.dot_general(a, b, (((1,), (1,)), ((), ())), preferred_element_type=F32)


def _dot_tn(a, b):
    return lax.dot_general(a, b, (((0,), (0,)), ((), ())), preferred_element_type=F32)


def _bdot(a, b):
    return _dot(a.astype(BF16), b)


def _headsum(z, bd):
    hi = z.astype(BF16)
    lo = (z - hi.astype(F32)).astype(BF16)
    return _dot(hi, bd) + _dot(lo, bd)


def _layer_norm(x, g, b):
    mu = jnp.mean(x, axis=-1, keepdims=True)
    d = x - mu
    var = jnp.mean(d * d, axis=-1, keepdims=True)
    return d * lax.rsqrt(var + LN_EPS) * g + b


def _sigmoid(x):
    return 1.0 / (1.0 + jnp.exp(-x))


def _softplus(x):
    return jnp.maximum(x, 0.0) + jnp.log(1.0 + jnp.exp(-jnp.abs(x)))


def _shift_rows(x, prev_rows, n):
    out = pltpu.roll(x, n, axis=0)
    row = lax.broadcasted_iota(jnp.int32, x.shape, 0)
    p = prev_rows.shape[0]
    for i in range(n):
        out = jnp.where(row == i, prev_rows[p - n + i:p - n + i + 1, :], out)
    return out


def _params(*sem):
    return pltpu.CompilerParams(dimension_semantics=sem, vmem_limit_bytes=VMEM_LIMIT)


def _const_spec(shape):
    nd = len(shape)
    return pl.BlockSpec(shape, lambda *_: (0,) * nd)


def _rwkv_pre_kernel(*refs, has_vlora):
    if has_vlora:
        (x_ref, xp_ref, vf_ref, mix_ref, vec_ref, wrkv_ref, w1_ref, w2_ref, a1_ref, a2_ref, g1_ref, g2_ref,
         v1_ref, v2_ref, bd_ref, r_out, w_out, k_out, v_out, a_out, b_out, g_out) = refs
    else:
        (x_ref, xp_ref, mix_ref, vec_ref, wrkv_ref, w1_ref, w2_ref, a1_ref, a2_ref, g1_ref, g2_ref,
         bd_ref, r_out, w_out, k_out, v_out, a_out, b_out, g_out) = refs
    t = pl.program_id(1)
    x = x_ref[0]
    prev = jnp.where(t == 0, 0.0, xp_ref[0])
    xx = _shift_rows(x, prev, 1) - x
    mix = mix_ref[...]
    vec = vec_ref[...]
    w0, a0, v0, k_k, k_a = (vec[i:i + 1, :] for i in range(5))

    def mixed(i):
        return (x + xx * mix[i:i + 1, :]).astype(BF16)

    xv = mixed(2)
    r = _dot(mixed(0), wrkv_ref[0])
    k = _dot(mixed(1), wrkv_ref[1])
    v = _dot(xv, wrkv_ref[2])
    ww = w0 + _bdot(jnp.tanh(_dot(mixed(3), w1_ref[...])), w2_ref[...])
    logw = -jnp.exp(-_softplus(-ww) - 0.5)
    a = _sigmoid(a0 + _bdot(_dot(mixed(4), a1_ref[...]), a2_ref[...]))
    g = _bdot(_sigmoid(_dot(mixed(5), g1_ref[...])), g2_ref[...])
    if has_vlora:
        v = v + (vf_ref[0] - v) * _sigmoid(v0 + _bdot(_dot(xv, v1_ref[...]), v2_ref[...]))
    kk = k * k_k
    kk = kk / jnp.maximum(jnp.sqrt(_headsum(kk * kk, bd_ref[...])), 1e-12)
    r_out[0] = r
    w_out[0] = logw
    k_out[0] = k * (1.0 + (a - 1.0) * k_a)
    v_out[0] = v
    a_out[0] = -kk
    b_out[0] = kk * a
    g_out[0] = g


def _rwkv_pre(x, v_first, mix, vec, wrkv, w1, w2, a1, a2, g1, g2, v1, v2, bd, tm):
    bsz, t, d = x.shape
    has_vlora = v_first is not None
    tok = pl.BlockSpec((1, tm, d), lambda b, i: (b, i, 0))
    prev = pl.BlockSpec((1, 8, d), lambda b, i: (b, jnp.maximum(i * (tm // 8) - 1, 0), 0))
    ins = [x, x] + ([v_first] if has_vlora else []) + [mix, vec, wrkv, w1, w2, a1, a2, g1, g2]
    specs = [tok, prev] + ([tok] if has_vlora else []) + [_const_spec(z.shape) for z in
                                                          (mix, vec, wrkv, w1, w2, a1, a2, g1, g2)]
    if has_vlora:
        ins += [v1, v2]
        specs += [_const_spec(v1.shape), _const_spec(v2.shape)]
    ins.append(bd)
    specs.append(_const_spec(bd.shape))
    out = jax.ShapeDtypeStruct((bsz, t, d), F32)
    return pl.pallas_call(
        functools.partial(_rwkv_pre_kernel, has_vlora=has_vlora),
        grid=(bsz, t // tm),
        in_specs=specs,
        out_specs=[tok] * 7,
        out_shape=[out] * 7,
        compiler_params=_params("arbitrary", "arbitrary"),
        name="rwkv_pre",
    )(*ins)


def _wkv_kernel(r_ref, w_ref, k_ref, v_ref, a_ref, b_ref, y_ref, h_ref):
    n_pairs = h_ref.shape[0]
    L = WKV_CHUNK

    @pl.when(pl.program_id(1) == 0)
    def _():
        h_ref[...] = jnp.zeros_like(h_ref)

    row = lax.broadcasted_iota(jnp.int32, (2 * L, 2 * L), 0)
    col = lax.broadcasted_iota(jnp.int32, (2 * L, 2 * L), 1)
    strict = row > col
    incl = row >= col
    eye = (row == col).astype(F32)
    tri = (lax.broadcasted_iota(jnp.int32, (L, L), 0) >= lax.broadcasted_iota(jnp.int32, (L, L), 1)).astype(F32)
    first_head = lax.broadcasted_iota(jnp.int32, (L, LANES), 1) < HEAD_SIZE

    def stack(z):
        return jnp.concatenate([jnp.where(first_head, z, 0.0), jnp.where(first_head, 0.0, z)], axis=0)

    for p in range(n_pairs):
        sl = slice(p * LANES, (p + 1) * LANES)
        logw = w_ref[0, :, sl]
        c = jnp.dot(tri, logw, precision=lax.Precision.HIGHEST, preferred_element_type=F32)
        g_inc = jnp.exp(c)
        g_inv = jnp.exp(-c)
        a2 = stack(a_ref[0, :, sl] * jnp.exp(c - logw))
        r2 = stack(r_ref[0, :, sl] * g_inc)
        b2 = stack(b_ref[0, :, sl] * g_inv).astype(BF16)
        k2 = stack(k_ref[0, :, sl] * g_inv).astype(BF16)
        v2 = stack(v_ref[0, :, sl]).astype(BF16)
        gram = _dot_nt(jnp.concatenate([a2, r2], axis=0).astype(BF16), jnp.concatenate([b2, k2], axis=0))
        n = 2 * L
        a_ab = jnp.where(strict, gram[:n, :n], 0.0)
        a_ak = jnp.where(strict, gram[:n, n:], 0.0)
        a_rb = jnp.where(incl, gram[n:, :n], 0.0)
        a_rk = jnp.where(incl, gram[n:, n:], 0.0)
        inv = eye + a_ab
        q = a_ab
        for _ in range(L.bit_length() - 2):
            qb = q.astype(BF16)
            q = _dot(qb, qb)
            inv = inv + _bdot(inv, q.astype(BF16))
        x1 = _bdot(a_ak, v2)
        wu = _bdot(inv, jnp.concatenate([a2, x1], axis=1).astype(BF16))
        wub = wu.astype(BF16)
        bt = _dot_tn(b2, wub)
        kt = _dot_tn(k2, v2)
        g_col = jnp.transpose(jnp.broadcast_to(g_inc[L - 1:L, :], (LANES, LANES)))
        m_c = g_col * (eye + bt[:, :LANES])
        n_c = g_col * (bt[:, LANES:] + kt)
        qy = _bdot(a_rb, wub)
        q2 = r2 + qy[:, :LANES]
        yv2 = qy[:, LANES:] + _bdot(a_rk, v2)
        q_c = q2[:L] + q2[L:]
        y_v = yv2[:L] + yv2[L:]
        h = h_ref[p]
        hb = h.astype(BF16)
        y_ref[0, :, sl] = _bdot(q_c, hb) + y_v
        h_ref[p] = _bdot(m_c, hb) + n_c


def _wkv(r, logw, k, v, a, b):
    bsz, t, d = r.shape
    blk = pl.BlockSpec((1, WKV_CHUNK, d), lambda i, c: (i, c, 0))
    return pl.pallas_call(
        _wkv_kernel,
        grid=(bsz, t // WKV_CHUNK),
        in_specs=[blk] * 6,
        out_specs=blk,
        out_shape=jax.ShapeDtypeStruct((bsz, t, d), F32),
        scratch_shapes=[pltpu.VMEM((d // LANES, LANES, LANES), F32)],
        compiler_params=_params("arbitrary", "arbitrary"),
        name="wkv_scan",
    )(r, logw, k, v, a, b)


def _rwkv_post_kernel(y_ref, r_ref, k_ref, v_ref, g_ref, x_ref, vec_ref, wo_ref, bd_ref, o_ref):
    vec = vec_ref[...]
    r_k, lnx_g, lnx_b, ln_g, ln_b = (vec[i:i + 1, :] for i in range(5))
    bd = bd_ref[...]
    y = y_ref[...]
    mu = _headsum(y, bd) * (1.0 / HEAD_SIZE)
    dy = y - mu
    var = _headsum(dy * dy, bd) * (1.0 / HEAD_SIZE)
    yn = dy * lax.rsqrt(var + GN_EPS) * lnx_g + lnx_b
    bonus = _headsum(r_ref[...] * k_ref[...] * r_k, bd) * v_ref[...]
    mixed = _bdot((yn + bonus) * g_ref[...], wo_ref[...])
    o_ref[...] = _layer_norm(DEEPNORM_ALPHA * x_ref[...] + mixed, ln_g, ln_b)


def _rwkv_post(y, r, k, v, g, x2, vec, wo, bd, tm):
    n, d = x2.shape
    tok = pl.BlockSpec((tm, d), lambda i: (i, 0))
    return pl.pallas_call(
        _rwkv_post_kernel,
        grid=(n // tm,),
        in_specs=[tok] * 6 + [_const_spec(vec.shape), _const_spec(wo.shape), _const_spec(bd.shape)],
        out_specs=tok,
        out_shape=jax.ShapeDtypeStruct((n, d), F32),
        compiler_params=_params("arbitrary"),
        name="rwkv_post",
    )(y, r, k, v, g, x2, vec, wo, bd)


def _conv_kernel(x_ref, xp_ref, win_ref, cw_ref, wout_ref, vec_ref, o_ref):
    d = x_ref.shape[-1]
    t = pl.program_id(1)
    x = x_ref[0]
    proj = _bdot(x, win_ref[...])
    gate_b = proj[:, :d]
    ch = proj[:, d:2 * d] * proj[:, 2 * d:]
    pproj = _bdot(xp_ref[0], win_ref[:, d:])
    ch_prev = jnp.where(t == 0, 0.0, pproj[:, :d] * pproj[:, d:])
    cw = cw_ref[...]
    u = cw[0:1, :] * _shift_rows(ch, ch_prev, 2) + cw[1:2, :] * _shift_rows(ch, ch_prev, 1) + cw[2:3, :] * ch
    mixed = _bdot(gate_b * u, wout_ref[...])
    vec = vec_ref[...]
    o_ref[0] = _layer_norm(DEEPNORM_ALPHA * x + mixed, vec[0:1, :], vec[1:2, :])


def _conv_mix(x, win, cw, wout, vec, tm):
    bsz, t, d = x.shape
    tok = pl.BlockSpec((1, tm, d), lambda b, i: (b, i, 0))
    prev = pl.BlockSpec((1, 8, d), lambda b, i: (b, jnp.maximum(i * (tm // 8) - 1, 0), 0))
    return pl.pallas_call(
        _conv_kernel,
        grid=(bsz, t // tm),
        in_specs=[tok, prev] + [_const_spec(z.shape) for z in (win, cw, wout, vec)],
        out_specs=tok,
        out_shape=jax.ShapeDtypeStruct((bsz, t, d), F32),
        compiler_params=_params("arbitrary", "arbitrary"),
        name="conv_mix",
    )(x, x, win, cw, wout, vec)


def _router_kernel(x_ref, w_ref, b_ref, idx_ref, gate_ref):
    logits = jnp.dot(x_ref[...], w_ref[...], precision=lax.Precision.HIGHEST, preferred_element_type=F32) + b_ref[...]
    lane = lax.broadcasted_iota(jnp.int32, logits.shape, 1)
    logits = jnp.where(lane < N_EXPERTS, logits, -jnp.inf)
    vals, idxs = [], []
    for _ in range(TOP_K):
        m = jnp.max(logits, axis=-1, keepdims=True)
        sel = jnp.min(jnp.where(logits == m, lane, LANES), axis=-1, keepdims=True)
        vals.append(m)
        idxs.append(sel)
        logits = jnp.where(lane == sel, -jnp.inf, logits)
    es = [jnp.exp(m - vals[0]) for m in vals]
    tot = es[0] + es[1] + es[2] + es[3]
    idx_out = jnp.zeros(logits.shape, jnp.int32)
    gate_out = jnp.zeros(logits.shape, F32)
    for i in range(TOP_K):
        idx_out = jnp.where(lane == i, idxs[i], idx_out)
        gate_out = jnp.where(lane == i, es[i] / tot, gate_out)
    idx_ref[...] = idx_out
    gate_ref[...] = gate_out


def _router(x2, w, b, tm):
    n, d = x2.shape
    tok = pl.BlockSpec((tm, d), lambda i: (i, 0))
    out = pl.BlockSpec((tm, LANES), lambda i: (i, 0))
    return pl.pallas_call(
        _router_kernel,
        grid=(n // tm,),
        in_specs=[tok, _const_spec(w.shape), _const_spec(b.shape)],
        out_specs=[out, out],
        out_shape=[jax.ShapeDtypeStruct((n, LANES), jnp.int32), jax.ShapeDtypeStruct((n, LANES), F32)],
        compiler_params=_params("arbitrary"),
        name="router",
    )(x2, w, b)


def _row_copy(src_hbm, row, buf, slot, j, sem):
    return pltpu.make_async_copy(src_hbm.at[pl.ds(row, 1), :], buf.at[slot, pl.ds(j, 1), :], sem.at[slot])


def _start_rows(src_hbm, idx_ref, buf, slot, sem, n_rows):
    def body(j, carry):
        _row_copy(src_hbm, idx_ref[0, 0, j], buf, slot, j, sem).start()
        return carry

    lax.fori_loop(0, n_rows, body, 0, unroll=8)


def _wait_rows(src_hbm, buf, slot, sem, n_rows):
    def body(j, carry):
        _row_copy(src_hbm, 0, buf, slot, 0, sem).wait()
        return carry

    lax.fori_loop(0, n_rows, body, 0, unroll=8)


def _gather_step(src_hbm, idx_ref, idx_next_ref, buf, sem, n_rows, n_active):
    i = pl.program_id(0)
    slot = i % 2

    @pl.when(i == 0)
    def _():
        _start_rows(src_hbm, idx_ref, buf, 0, sem, n_rows)

    @pl.when(i + 1 < n_active)
    def _():
        _start_rows(src_hbm, idx_next_ref, buf, 1 - slot, sem, n_rows)

    return slot


def _expert_kernel(be_ref, nact_ref, idx_ref, idxn_ref, x_hbm, wgu_ref, bgu_ref, wd_ref, bd_ref, o_ref, buf, sem):
    i = pl.program_id(0)
    tm = buf.shape[1]
    f = wd_ref.shape[1]
    n_active = nact_ref[0]
    slot = _gather_step(x_hbm, idx_ref, idxn_ref, buf, sem, tm, n_active)

    @pl.when(i < n_active)
    def _():
        _wait_rows(x_hbm, buf, slot, sem, tm)
        hcat = _bdot(buf[slot], wgu_ref[0]) + bgu_ref[0]
        glu = jnp.minimum(hcat[:, :f], SWIGLU_LIMIT)
        lin = jnp.clip(hcat[:, f:], -SWIGLU_LIMIT, SWIGLU_LIMIT)
        act = glu * _sigmoid(SWIGLU_ALPHA * glu) * (lin + 1.0)
        o_ref[...] = _bdot(act, wd_ref[0]) + bd_ref[0]

    @pl.when(i >= n_active)
    def _():
        o_ref[...] = jnp.zeros_like(o_ref)


def _experts(x2, block_e, n_active, src_tok, wgu, bgu, wd, bdn, tm):
    n, d = x2.shape
    n_blocks = block_e.shape[0]
    f2 = wgu.shape[-1]
    f = wd.shape[1]
    idx = src_tok.reshape(n_blocks, 1, tm)
    grid_spec = pltpu.PrefetchScalarGridSpec(
        num_scalar_prefetch=2,
        grid=(n_blocks,),
        in_specs=[
            pl.BlockSpec((1, 1, tm), lambda i, be, na: (i, 0, 0), memory_space=pltpu.SMEM),
            pl.BlockSpec((1, 1, tm), lambda i, be, na: (jnp.minimum(i + 1, n_blocks - 1), 0, 0),
                         memory_space=pltpu.SMEM),
            pl.BlockSpec(memory_space=pl.ANY),
            pl.BlockSpec((1, d, f2), lambda i, be, na: (be[i], 0, 0)),
            pl.BlockSpec((1, 1, f2), lambda i, be, na: (be[i], 0, 0)),
            pl.BlockSpec((1, f, d), lambda i, be, na: (be[i], 0, 0)),
            pl.BlockSpec((1, 1, d), lambda i, be, na: (be[i], 0, 0)),
        ],
        out_specs=pl.BlockSpec((tm, d), lambda i, be, na: (i, 0)),
        scratch_shapes=[pltpu.VMEM((2, tm, d), F32), pltpu.SemaphoreType.DMA((2,))],
    )
    return pl.pallas_call(
        _expert_kernel,
        grid_spec=grid_spec,
        out_shape=jax.ShapeDtypeStruct((n_blocks * tm, d), F32),
        compiler_params=_params("arbitrary"),
        name="experts",
    )(block_e, n_active, idx, idx, x2, wgu, bgu.reshape(N_EXPERTS, 1, f2), wd, bdn.reshape(N_EXPERTS, 1, d))


def _combine_kernel(idx_ref, idxn_ref, y_hbm, gate_ref, x_ref, p_ref, wproj_ref, wgate_ref, vec_ref, o_ref, buf, sem):
    tm = x_ref.shape[0]
    n_rows = TOP_K * tm
    slot = _gather_step(y_hbm, idx_ref, idxn_ref, buf, sem, n_rows, pl.num_programs(0))
    _wait_rows(y_hbm, buf, slot, sem, n_rows)
    vec = vec_ref[...]
    ffn_g, ffn_b, b_gate, ple_g, ple_b = (vec[i:i + 1, :] for i in range(5))
    gate = gate_ref[...]
    ffn = buf[slot, 0:tm, :] * gate[:, 0:1]
    for k in range(1, TOP_K):
        ffn = ffn + buf[slot, k * tm:(k + 1) * tm, :] * gate[:, k:k + 1]
    x = _layer_norm(DEEPNORM_ALPHA * x_ref[...] + ffn, ffn_g, ffn_b)
    ple = _bdot(p_ref[...], wproj_ref[...]) * _sigmoid(_bdot(x, wgate_ref[...]) + b_gate)
    o_ref[...] = _layer_norm(DEEPNORM_ALPHA * x + ple, ple_g, ple_b)


def _combine(y_rows, slots, gate, x2, p2, wproj, wgate, vec, tm):
    n, d = x2.shape
    n_tiles = n // tm
    idx = slots.reshape(n_tiles, tm, TOP_K).transpose(0, 2, 1).reshape(n_tiles, 1, TOP_K * tm)
    tok = pl.BlockSpec((tm, d), lambda i: (i, 0))
    return pl.pallas_call(
        _combine_kernel,
        grid=(n_tiles,),
        in_specs=[
            pl.BlockSpec((1, 1, TOP_K * tm), lambda i: (i, 0, 0), memory_space=pltpu.SMEM),
            pl.BlockSpec((1, 1, TOP_K * tm), lambda i: (jnp.minimum(i + 1, n_tiles - 1), 0, 0),
                         memory_space=pltpu.SMEM),
            pl.BlockSpec(memory_space=pl.ANY),
            pl.BlockSpec((tm, LANES), lambda i: (i, 0)),
            tok,
            pl.BlockSpec((tm, p2.shape[1]), lambda i: (i, 0)),
            _const_spec(wproj.shape),
            _const_spec(wgate.shape),
            _const_spec(vec.shape),
        ],
        out_specs=tok,
        out_shape=jax.ShapeDtypeStruct((n, d), F32),
        scratch_shapes=[pltpu.VMEM((2, TOP_K * tm, d), F32), pltpu.SemaphoreType.DMA((2,))],
        compiler_params=_params("arbitrary"),
        name="combine_ple",
    )(idx, idx, y_rows, gate, x2, p2, wproj, wgate, vec)


def _routing(top_idx, tm):
    n = top_idx.shape[0]
    n_assign = n * TOP_K
    flat_e = top_idx.reshape(n_assign)
    order = jnp.argsort(flat_e)
    sorted_e = flat_e[order]
    counts = jnp.bincount(flat_e, length=N_EXPERTS)
    padded = (counts + tm - 1) // tm * tm
    pad_end = jnp.cumsum(padded)
    pad_start = pad_end - padded
    start = jnp.cumsum(counts) - counts
    dest = (pad_start[sorted_e] + jnp.arange(n_assign, dtype=jnp.int32) - start[sorted_e]).astype(jnp.int32)
    n_blocks = n_assign // tm + N_EXPERTS
    src_tok = jnp.zeros((n_blocks * tm,), jnp.int32).at[dest].set((order // TOP_K).astype(jnp.int32))
    slots = jnp.zeros((n_assign,), jnp.int32).at[order].set(dest).reshape(n, TOP_K)
    block_e = jnp.minimum(jnp.searchsorted(pad_end, jnp.arange(n_blocks) * tm, side='right'),
                          N_EXPERTS - 1).astype(jnp.int32)
    n_active = (pad_end[-1:] // tm).astype(jnp.int32)
    return src_tok, slots, block_e, n_active


def _tile(n, want):
    t = min(n, want)
    assert n % t == 0 and t % 8 == 0, (n, t)
    return t


def kernel(x, p, rwkv_mix, rwkv_w_rkv, rwkv_w0, rwkv_w1, rwkv_w2, rwkv_a0, rwkv_a1, rwkv_a2, rwkv_v0, rwkv_v1, rwkv_v2, rwkv_g1, rwkv_g2, rwkv_k_k, rwkv_k_a, rwkv_r_k, rwkv_lnx_g, rwkv_lnx_b, rwkv_w_o, conv_w_in, conv_w, conv_w_out, ln_mix_g, ln_mix_b, router_w, router_b, moe_w_gu, moe_b_gu, moe_w_down, moe_b_down, ln_ffn_g, ln_ffn_b, ple_w_proj, ple_w_gate, ple_b_gate, ln_ple_g, ln_ple_b):
    bsz, t, d = x.shape
    n = bsz * t
    assert d % LANES == 0 and t % WKV_CHUNK == 0 and (n * TOP_K) % 512 == 0
    tm_seq = _tile(t, 256)
    tm_tok = _tile(n, 256)
    tm_exp = 512
    head = jnp.arange(d, dtype=jnp.int32) // HEAD_SIZE
    bd = (head[:, None] == head[None, :]).astype(BF16)
    bf = lambda z: z.astype(BF16)
    zeros = jnp.zeros((d,), F32)
    router_wp = jnp.zeros((DEPTH, d, LANES), F32).at[:, :, :N_EXPERTS].set(router_w)
    router_bp = jnp.zeros((DEPTH, 1, LANES), F32).at[:, 0, :N_EXPERTS].set(router_b)

    v_first = None
    for i in range(DEPTH):
        j = i // 2
        if i % 2 == 0:
            has_vlora = j > 0
            vec = jnp.stack([rwkv_w0[j], rwkv_a0[j], rwkv_v0[j - 1] if has_vlora else zeros, rwkv_k_k[j],
                             rwkv_k_a[j], zeros, zeros, zeros])
            r, logw, k, v, a, b, g = _rwkv_pre(
                x, v_first, rwkv_mix[j], vec, bf(rwkv_w_rkv[j]), bf(rwkv_w1[j]), bf(rwkv_w2[j]), bf(rwkv_a1[j]),
                bf(rwkv_a2[j]), bf(rwkv_g1[j]), bf(rwkv_g2[j]),
                bf(rwkv_v1[j - 1]) if has_vlora else None, bf(rwkv_v2[j - 1]) if has_vlora else None, bd, tm_seq)
            if not has_vlora:
                v_first = v
            y = _wkv(r, logw, k, v, a, b)
            vec = jnp.stack([rwkv_r_k[j].reshape(d), rwkv_lnx_g[j], rwkv_lnx_b[j], ln_mix_g[i], ln_mix_b[i],
                             zeros, zeros, zeros])
            flat = lambda z: z.reshape(n, d)
            x2 = _rwkv_post(flat(y), flat(r), flat(k), flat(v), flat(g), flat(x), vec, bf(rwkv_w_o[j]), bd, tm_tok)
        else:
            vec = jnp.stack([ln_mix_g[i], ln_mix_b[i]] + [zeros] * 6)
            x2 = _conv_mix(x, bf(conv_w_in[j]), conv_w[j], bf(conv_w_out[j]), vec, tm_seq).reshape(n, d)
        top_idx, gate = _router(x2, router_wp[i], router_bp[i], tm_tok)
        src_tok, slots, block_e, n_active = _routing(top_idx[:, :TOP_K], tm_exp)
        y_rows = _experts(x2, block_e, n_active, src_tok, bf(moe_w_gu[i]), moe_b_gu[i], bf(moe_w_down[i]),
                          moe_b_down[i], tm_exp)
        vec = jnp.stack([ln_ffn_g[i], ln_ffn_b[i], ple_b_gate[i], ln_ple_g[i], ln_ple_b[i], zeros, zeros, zeros])
        x = _combine(y_rows, slots, gate, x2, p[i].reshape(n, -1), bf(ple_w_proj[i]), bf(ple_w_gate[i]), vec,
                     tm_tok).reshape(bsz, t, d)
    return x
```

```python
import functools

import jax
import jax.numpy as jnp
from jax import lax
from jax.experimental import pallas as pl
from jax.experimental.pallas import tpu as pltpu

DEPTH = 4
HEAD_SIZE = 64
LANES = 128
SUBLANES = 8
N_EXPERTS = 32
TOP_K = 4
GN_EPS = HEAD_SIZE * 1e-5
LN_EPS = 1e-5
SWIGLU_LIMIT = 7.0
SWIGLU_ALPHA = 1.702
DEEPNORM_ALPHA = (2.0 * DEPTH) ** 0.25
WKV_CHUNK = 64
VMEM_LIMIT = 56 * 1024 * 1024

F32 = jnp.float32
BF16 = jnp.bfloat16


def _dot(a, b):
    return jnp.dot(a, b, preferred_element_type=F32)


def _dot_nt(a, b):
    return lax.dot_general(a, b, (((1,), (1,)), ((), ())), preferred_element_type=F32)


def _dot_tn(a, b):
    return lax.dot_general(a, b, (((0,), (0,)), ((), ())), preferred_element_type=F32)


def _bdot(a, b):
    return _dot(a.astype(BF16), b)


def _headsum(z, bd):
    hi = z.astype(BF16)
    lo = (z - hi.astype(F32)).astype(BF16)
    return _dot(hi, bd) + _dot(lo, bd)


def _layer_norm(x, g, b):
    mu = jnp.mean(x, axis=-1, keepdims=True)
    d = x - mu
    var = jnp.mean(d * d, axis=-1, keepdims=True)
    return d * lax.rsqrt(var + LN_EPS) * g + b


def _sigmoid(x):
    return 1.0 / (1.0 + jnp.exp(-x))


def _softplus(x):
    return jnp.maximum(x, 0.0) + jnp.log(1.0 + jnp.exp(-jnp.abs(x)))


def _shift_rows(x, prev_rows, n):
    out = pltpu.roll(x, n, axis=0)
    row = lax.broadcasted_iota(jnp.int32, x.shape, 0)
    p = prev_rows.shape[0]
    for i in range(n):
        out = jnp.where(row == i, prev_rows[p - n + i:p - n + i + 1, :], out)
    return out


def _params(*sem):
    return pltpu.CompilerParams(dimension_semantics=sem, vmem_limit_bytes=VMEM_LIMIT)


def _const_spec(shape):
    nd = len(shape)
    return pl.BlockSpec(shape, lambda *_: (0,) * nd)


def _rows3(z):
    return z.reshape(z.shape[0] // SUBLANES, SUBLANES, z.shape[1])


def _row_of(ref3, row):
    return ref3.at[lax.shift_right_logical(row, 3), pl.ds(row & (SUBLANES - 1), 1), :]


def _rwkv_pre_kernel(*refs, has_vlora):
    if has_vlora:
        (x_ref, xp_ref, vf_ref, mix_ref, vec_ref, wrkv_ref, w1_ref, w2_ref, a1_ref, a2_ref, g1_ref, g2_ref,
         v1_ref, v2_ref, bd_ref, r_out, w_out, k_out, v_out, a_out, b_out, g_out) = refs
    else:
        (x_ref, xp_ref, mix_ref, vec_ref, wrkv_ref, w1_ref, w2_ref, a1_ref, a2_ref, g1_ref, g2_ref,
         bd_ref, r_out, w_out, k_out, v_out, a_out, b_out, g_out) = refs
    t = pl.program_id(1)
    x = x_ref[0]
    prev = jnp.where(t == 0, 0.0, xp_ref[0])
    xx = _shift_rows(x, prev, 1) - x
    mix = mix_ref[...]
    vec = vec_ref[...]
    w0, a0, v0, k_k, k_a = (vec[i:i + 1, :] for i in range(5))

    def mixed(i):
        return (x + xx * mix[i:i + 1, :]).astype(BF16)

    xv = mixed(2)
    r = _dot(mixed(0), wrkv_ref[0])
    k = _dot(mixed(1), wrkv_ref[1])
    v = _dot(xv, wrkv_ref[2])
    ww = w0 + _bdot(jnp.tanh(_dot(mixed(3), w1_ref[...])), w2_ref[...])
    logw = -jnp.exp(-_softplus(-ww) - 0.5)
    a = _sigmoid(a0 + _bdot(_dot(mixed(4), a1_ref[...]), a2_ref[...]))
    g = _bdot(_sigmoid(_dot(mixed(5), g1_ref[...])), g2_ref[...])
    if has_vlora:
        v = v + (vf_ref[0] - v) * _sigmoid(v0 + _bdot(_dot(xv, v1_ref[...]), v2_ref[...]))
    kk = k * k_k
    kk = kk / jnp.maximum(jnp.sqrt(_headsum(kk * kk, bd_ref[...])), 1e-12)
    r_out[0] = r
    w_out[0] = logw
    k_out[0] = k * (1.0 + (a - 1.0) * k_a)
    v_out[0] = v
    a_out[0] = -kk
    b_out[0] = kk * a
    g_out[0] = g


def _rwkv_pre(x, v_first, mix, vec, wrkv, w1, w2, a1, a2, g1, g2, v1, v2, bd, tm):
    bsz, t, d = x.shape
    has_vlora = v_first is not None
    tok = pl.BlockSpec((1, tm, d), lambda b, i: (b, i, 0))
    prev = pl.BlockSpec((1, 8, d), lambda b, i: (b, jnp.maximum(i * (tm // 8) - 1, 0), 0))
    ins = [x, x] + ([v_first] if has_vlora else []) + [mix, vec, wrkv, w1, w2, a1, a2, g1, g2]
    specs = [tok, prev] + ([tok] if has_vlora else []) + [_const_spec(z.shape) for z in
                                                          (mix, vec, wrkv, w1, w2, a1, a2, g1, g2)]
    if has_vlora:
        ins += [v1, v2]
        specs += [_const_spec(v1.shape), _const_spec(v2.shape)]
    ins.append(bd)
    specs.append(_const_spec(bd.shape))
    out = jax.ShapeDtypeStruct((bsz, t, d), F32)
    return pl.pallas_call(
        functools.partial(_rwkv_pre_kernel, has_vlora=has_vlora),
        grid=(bsz, t // tm),
        in_specs=specs,
        out_specs=[tok] * 7,
        out_shape=[out] * 7,
        compiler_params=_params("arbitrary", "arbitrary"),
        name="rwkv_pre",
    )(*ins)


def _wkv_kernel(r_ref, w_ref, k_ref, v_ref, a_ref, b_ref, y_ref, s_ref):
    pairs = range(s_ref.shape[0])
    L = WKV_CHUNK
    n = 2 * L

    @pl.when(pl.program_id(1) == 0)
    def _():
        s_ref[...] = jnp.zeros_like(s_ref)

    row = lax.broadcasted_iota(jnp.int32, (n, n), 0)
    col = lax.broadcasted_iota(jnp.int32, (n, n), 1)
    strict = row > col
    incl = row >= col
    eye = (row == col).astype(F32)
    tri = (lax.broadcasted_iota(jnp.int32, (L, L), 0) >= lax.broadcasted_iota(jnp.int32, (L, L), 1)).astype(F32)
    first_head = lax.broadcasted_iota(jnp.int32, (L, LANES), 1) < HEAD_SIZE

    def stack(z):
        return jnp.concatenate([jnp.where(first_head, z, 0.0), jnp.where(first_head, 0.0, z)], axis=0)

    def lanes(ref, p):
        return ref[0, :, p * LANES:(p + 1) * LANES]

    logw = [lanes(w_ref, p) for p in pairs]
    c = [jnp.dot(tri, lw, precision=lax.Precision.HIGHEST, preferred_element_type=F32) for lw in logw]
    g_inc = [jnp.exp(ci) for ci in c]
    g_inv = [jnp.exp(-ci) for ci in c]
    a2 = [stack(lanes(a_ref, p) * jnp.exp(c[p] - logw[p])) for p in pairs]
    r2 = [stack(lanes(r_ref, p) * g_inc[p]) for p in pairs]
    b2 = [stack(lanes(b_ref, p) * g_inv[p]).astype(BF16) for p in pairs]
    k2 = [stack(lanes(k_ref, p) * g_inv[p]).astype(BF16) for p in pairs]
    v2 = [stack(lanes(v_ref, p)).astype(BF16) for p in pairs]
    gram = [_dot_nt(jnp.concatenate([a2[p], r2[p]], axis=0).astype(BF16), jnp.concatenate([b2[p], k2[p]], axis=0))
            for p in pairs]
    a_ab = [jnp.where(strict, gm[:n, :n], 0.0) for gm in gram]
    a_ak = [jnp.where(strict, gm[:n, n:], 0.0) for gm in gram]
    a_rb = [jnp.where(incl, gm[n:, :n], 0.0).astype(BF16) for gm in gram]
    a_rk = [jnp.where(incl, gm[n:, n:], 0.0).astype(BF16) for gm in gram]
    x1 = [_bdot(a_ak[p], v2[p]) for p in pairs]
    inv = [eye + m for m in a_ab]
    q = a_ab
    for _ in range(L.bit_length() - 2):
        qb = [m.astype(BF16) for m in q]
        q = [_dot(m, m) for m in qb]
        inv = [inv[p] + _bdot(inv[p], q[p].astype(BF16)) for p in pairs]
    wu = [_bdot(inv[p], jnp.concatenate([a2[p], x1[p]], axis=1).astype(BF16)).astype(BF16) for p in pairs]
    zt = [_dot_tn(wu[p], b2[p]) for p in pairs]
    vk = [_dot_tn(v2[p], k2[p]) for p in pairs]
    qy = [_dot(a_rb[p], wu[p]) for p in pairs]
    yv2 = [qy[p][:, LANES:] + _dot(a_rk[p], v2[p]) for p in pairs]
    for p in pairs:
        q2 = r2[p] + qy[p][:, :LANES]
        q_c = q2[:L] + q2[L:]
        y_v = yv2[p][:L] + yv2[p][L:]
        s = s_ref[p]
        sb = s.astype(BF16)
        y_ref[0, :, p * LANES:(p + 1) * LANES] = _dot_nt(q_c.astype(BF16), sb) + y_v
        s_ref[p] = (s + _bdot(sb, zt[p][:LANES].astype(BF16)) + zt[p][LANES:] + vk[p]) * g_inc[p][L - 1:L, :]


def _wkv(r, logw, k, v, a, b):
    bsz, t, d = r.shape
    blk = pl.BlockSpec((1, WKV_CHUNK, d), lambda i, c: (i, c, 0))
    return pl.pallas_call(
        _wkv_kernel,
        grid=(bsz, t // WKV_CHUNK),
        in_specs=[blk] * 6,
        out_specs=blk,
        out_shape=jax.ShapeDtypeStruct((bsz, t, d), F32),
        scratch_shapes=[pltpu.VMEM((d // LANES, LANES, LANES), F32)],
        compiler_params=_params("arbitrary", "arbitrary"),
        name="wkv_scan",
    )(r, logw, k, v, a, b)


def _rwkv_post_kernel(y_ref, r_ref, k_ref, v_ref, g_ref, x_ref, vec_ref, wo_ref, bd_ref, o_ref):
    vec = vec_ref[...]
    r_k, lnx_g, lnx_b, ln_g, ln_b = (vec[i:i + 1, :] for i in range(5))
    bd = bd_ref[...]
    y = y_ref[...]
    mu = _headsum(y, bd) * (1.0 / HEAD_SIZE)
    dy = y - mu
    var = _headsum(dy * dy, bd) * (1.0 / HEAD_SIZE)
    yn = dy * lax.rsqrt(var + GN_EPS) * lnx_g + lnx_b
    bonus = _headsum(r_ref[...] * k_ref[...] * r_k, bd) * v_ref[...]
    mixed = _bdot((yn + bonus) * g_ref[...], wo_ref[...])
    o_ref[...] = _layer_norm(DEEPNORM_ALPHA * x_ref[...] + mixed, ln_g, ln_b)


def _rwkv_post(y, r, k, v, g, x2, vec, wo, bd, tm):
    n, d = x2.shape
    tok = pl.BlockSpec((tm, d), lambda i: (i, 0))
    return pl.pallas_call(
        _rwkv_post_kernel,
        grid=(n // tm,),
        in_specs=[tok] * 6 + [_const_spec(vec.shape), _const_spec(wo.shape), _const_spec(bd.shape)],
        out_specs=tok,
        out_shape=jax.ShapeDtypeStruct((n, d), F32),
        compiler_params=_params("arbitrary"),
        name="rwkv_post",
    )(y, r, k, v, g, x2, vec, wo, bd)


def _conv_kernel(x_ref, xp_ref, win_ref, cw_ref, wout_ref, vec_ref, o_ref):
    d = x_ref.shape[-1]
    t = pl.program_id(1)
    x = x_ref[0]
    proj = _bdot(x, win_ref[...])
    gate_b = proj[:, :d]
    ch = proj[:, d:2 * d] * proj[:, 2 * d:]
    pproj = _bdot(xp_ref[0], win_ref[:, d:])
    ch_prev = jnp.where(t == 0, 0.0, pproj[:, :d] * pproj[:, d:])
    cw = cw_ref[...]
    u = cw[0:1, :] * _shift_rows(ch, ch_prev, 2) + cw[1:2, :] * _shift_rows(ch, ch_prev, 1) + cw[2:3, :] * ch
    mixed = _bdot(gate_b * u, wout_ref[...])
    vec = vec_ref[...]
    o_ref[0] = _layer_norm(DEEPNORM_ALPHA * x + mixed, vec[0:1, :], vec[1:2, :])


def _conv_mix(x, win, cw, wout, vec, tm):
    bsz, t, d = x.shape
    tok = pl.BlockSpec((1, tm, d), lambda b, i: (b, i, 0))
    prev = pl.BlockSpec((1, 8, d), lambda b, i: (b, jnp.maximum(i * (tm // 8) - 1, 0), 0))
    return pl.pallas_call(
        _conv_kernel,
        grid=(bsz, t // tm),
        in_specs=[tok, prev] + [_const_spec(z.shape) for z in (win, cw, wout, vec)],
        out_specs=tok,
        out_shape=jax.ShapeDtypeStruct((bsz, t, d), F32),
        compiler_params=_params("arbitrary", "arbitrary"),
        name="conv_mix",
    )(x, x, win, cw, wout, vec)


def _router_kernel(x_ref, w_ref, b_ref, route_ref, gate_ref, cnt_ref, base_ref):
    @pl.when(pl.program_id(0) == 0)
    def _():
        base_ref[...] = jnp.zeros_like(base_ref)

    logits = jnp.dot(x_ref[...], w_ref[...], precision=lax.Precision.HIGHEST, preferred_element_type=F32) + b_ref[...]
    tm = logits.shape[0]
    lane = lax.broadcasted_iota(jnp.int32, logits.shape, 1)
    logits = jnp.where(lane < N_EXPERTS, logits, -jnp.inf)
    vals, idxs = [], []
    for _ in range(TOP_K):
        m = jnp.max(logits, axis=-1, keepdims=True)
        sel = jnp.min(jnp.where(logits == m, lane, LANES), axis=-1, keepdims=True)
        vals.append(m)
        idxs.append(sel)
        logits = jnp.where(lane == sel, -jnp.inf, logits)
    es = [jnp.exp(m - vals[0]) for m in vals]
    tot = es[0] + es[1] + es[2] + es[3]
    chosen = [lane == sel for sel in idxs]
    onehot = jnp.zeros(logits.shape, F32)
    for ch in chosen:
        onehot = onehot + ch.astype(F32)
    before = (lax.broadcasted_iota(jnp.int32, (tm, tm), 0) > lax.broadcasted_iota(jnp.int32, (tm, tm), 1))
    pos_all = base_ref[0:1, :] + _dot(before.astype(BF16), onehot.astype(BF16))
    route = jnp.zeros(logits.shape, jnp.int32)
    gate_out = jnp.zeros(logits.shape, F32)
    for i in range(TOP_K):
        pos = jnp.sum(jnp.where(chosen[i], pos_all, 0.0), axis=-1, keepdims=True).astype(jnp.int32)
        route = jnp.where(lane == i, idxs[i], route)
        route = jnp.where(lane == TOP_K + i, pos, route)
        gate_out = jnp.where(lane == i, es[i] / tot, gate_out)
    route_ref[...] = route
    gate_ref[...] = gate_out
    total = base_ref[...] + jnp.sum(onehot, axis=0, keepdims=True)
    base_ref[...] = total
    cnt_ref[...] = total


def _router(x2, w, b, tm):
    n, d = x2.shape
    tok = pl.BlockSpec((tm, d), lambda i: (i, 0))
    out = pl.BlockSpec((tm, LANES), lambda i: (i, 0))
    cnt = pl.BlockSpec((SUBLANES, LANES), lambda i: (0, 0))
    return pl.pallas_call(
        _router_kernel,
        grid=(n // tm,),
        in_specs=[tok, _const_spec(w.shape), _const_spec(b.shape)],
        out_specs=[out, out, cnt],
        out_shape=[jax.ShapeDtypeStruct((n, LANES), jnp.int32), jax.ShapeDtypeStruct((n, LANES), F32),
                   jax.ShapeDtypeStruct((SUBLANES, LANES), F32)],
        scratch_shapes=[pltpu.VMEM((SUBLANES, LANES), F32)],
        compiler_params=_params("arbitrary"),
        name="router",
    )(x2, w, b)


def _dispatch_kernel(pe_ref, slot_ref, x_ref, xs_hbm, zero_ref, sem, zsem, *, tile_groups):
    groups = x_ref.shape[0]

    def zero_copy(e):
        start = lax.shift_right_logical(pe_ref[e], 3) - tile_groups
        return pltpu.make_async_copy(zero_ref, xs_hbm.at[pl.ds(start, tile_groups)], zsem)

    @pl.when(pl.program_id(0) == 0)
    def _():
        zero_ref[...] = jnp.zeros_like(zero_ref)
        for e in range(N_EXPERTS):
            @pl.when(pe_ref[e] > 0)
            def _():
                zero_copy(e).start()
        for e in range(N_EXPERTS):
            @pl.when(pe_ref[e] > 0)
            def _():
                zero_copy(e).wait()

    def row_copy(grp, u, slot):
        return pltpu.make_async_copy(x_ref.at[grp, pl.ds(u, 1), :], _row_of(xs_hbm, slot), sem)

    def start_group(grp, carry):
        for u in range(SUBLANES):
            for k in range(TOP_K):
                row_copy(grp, u, slot_ref[0, 0, (grp * SUBLANES + u) * TOP_K + k]).start(priority=k % 2)
        return carry

    def wait_group(grp, carry):
        for _ in range(SUBLANES * TOP_K):
            row_copy(0, 0, 0).wait()
        return carry

    lax.fori_loop(0, groups, start_group, 0)
    lax.fori_loop(0, groups, wait_group, 0)


def _dispatch(x2, slots, pad_end, n_rows, tm, tm_exp):
    n, d = x2.shape
    n_tiles = n // tm
    grid_spec = pltpu.PrefetchScalarGridSpec(
        num_scalar_prefetch=1,
        grid=(n_tiles,),
        in_specs=[
            pl.BlockSpec((1, 1, TOP_K * tm), lambda i, pe: (i, 0, 0), memory_space=pltpu.SMEM),
            pl.BlockSpec((tm // SUBLANES, SUBLANES, d), lambda i, pe: (i, 0, 0)),
        ],
        out_specs=pl.BlockSpec(memory_space=pl.ANY),
        scratch_shapes=[pltpu.VMEM((tm_exp // SUBLANES, SUBLANES, d), F32), pltpu.SemaphoreType.DMA(()),
                        pltpu.SemaphoreType.DMA(())],
    )
    xs = pl.pallas_call(
        functools.partial(_dispatch_kernel, tile_groups=tm_exp // SUBLANES),
        grid_spec=grid_spec,
        out_shape=jax.ShapeDtypeStruct((n_rows // SUBLANES, SUBLANES, d), F32),
        compiler_params=_params("arbitrary"),
        name="dispatch",
    )(pad_end, slots.reshape(n_tiles, 1, TOP_K * tm), _rows3(x2))
    return xs.reshape(n_rows, d)


def _expert_kernel(be_ref, nact_ref, x_ref, wgu_ref, bgu_ref, wd_ref, bd_ref, o_ref):
    f = wd_ref.shape[1]
    active = pl.program_id(0) < nact_ref[0]

    @pl.when(active)
    def _():
        hcat = _bdot(x_ref[...], wgu_ref[0]) + bgu_ref[0]
        glu = jnp.minimum(hcat[:, :f], SWIGLU_LIMIT)
        lin = jnp.clip(hcat[:, f:], -SWIGLU_LIMIT, SWIGLU_LIMIT)
        act = glu * _sigmoid(SWIGLU_ALPHA * glu) * (lin + 1.0)
        o_ref[...] = _bdot(act, wd_ref[0]) + bd_ref[0]

    @pl.when(jnp.logical_not(active))
    def _():
        o_ref[...] = jnp.zeros_like(o_ref)


def _experts(xs, block_e, n_active, wgu, bgu, wd, bdn, tm):
    n_rows, d = xs.shape
    n_blocks = n_rows // tm
    f2 = wgu.shape[-1]
    f = wd.shape[1]
    grid_spec = pltpu.PrefetchScalarGridSpec(
        num_scalar_prefetch=2,
        grid=(n_blocks,),
        in_specs=[
            pl.BlockSpec((tm, d), lambda i, be, na: (jnp.minimum(i, na[0] - 1), 0)),
            pl.BlockSpec((1, d, f2), lambda i, be, na: (be[i], 0, 0)),
            pl.BlockSpec((1, 1, f2), lambda i, be, na: (be[i], 0, 0)),
            pl.BlockSpec((1, f, d), lambda i, be, na: (be[i], 0, 0)),
            pl.BlockSpec((1, 1, d), lambda i, be, na: (be[i], 0, 0)),
        ],
        out_specs=pl.BlockSpec((tm, d), lambda i, be, na: (i, 0)),
    )
    return pl.pallas_call(
        _expert_kernel,
        grid_spec=grid_spec,
        out_shape=jax.ShapeDtypeStruct((n_rows, d), F32),
        compiler_params=_params("arbitrary"),
        name="experts",
    )(block_e, n_active, xs, wgu, bgu.reshape(N_EXPERTS, 1, f2), wd, bdn.reshape(N_EXPERTS, 1, d))


def _combine_kernel(idx_ref, idxn_ref, y_hbm, gate_ref, x_ref, p_ref, wproj_ref, wgate_ref, vec_ref, o_ref, buf, sem):
    tm = x_ref.shape[0]
    groups = tm // SUBLANES
    i = pl.program_id(0)
    slot = i % 2

    def row_copy(src_row, buf_slot, grp, u):
        return pltpu.make_async_copy(_row_of(y_hbm, src_row), buf.at[buf_slot, grp, pl.ds(u, 1), :],
                                     sem.at[buf_slot])

    def start_rows(ids_ref, buf_slot):
        def body(grp, carry):
            for u in range(SUBLANES):
                for k in range(TOP_K):
                    row_copy(ids_ref[0, 0, (grp * SUBLANES + u) * TOP_K + k], buf_slot, k * groups + grp,
                             u).start(priority=k % 2)
            return carry

        lax.fori_loop(0, groups, body, 0)

    @pl.when(i == 0)
    def _():
        start_rows(idx_ref, 0)

    @pl.when(i + 1 < pl.num_programs(0))
    def _():
        start_rows(idxn_ref, 1 - slot)

    def wait_body(grp, carry):
        for _ in range(SUBLANES * TOP_K):
            row_copy(0, slot, 0, 0).wait()
        return carry

    lax.fori_loop(0, groups, wait_body, 0)
    vec = vec_ref[...]
    ffn_g, ffn_b, b_gate, ple_g, ple_b = (vec[j:j + 1, :] for j in range(5))
    gate = gate_ref[...]
    d = x_ref.shape[1]
    ffn = None
    for k in range(TOP_K):
        part = buf[slot, k * groups:(k + 1) * groups].reshape(tm, d) * gate[:, k:k + 1]
        ffn = part if ffn is None else ffn + part
    x = _layer_norm(DEEPNORM_ALPHA * x_ref[...] + ffn, ffn_g, ffn_b)
    ple = _bdot(p_ref[...], wproj_ref[...]) * _sigmoid(_bdot(x, wgate_ref[...]) + b_gate)
    o_ref[...] = _layer_norm(DEEPNORM_ALPHA * x + ple, ple_g, ple_b)


def _combine(y_rows, slots, gate, x2, p2, wproj, wgate, vec, tm):
    n, d = x2.shape
    n_tiles = n // tm
    idx = slots.reshape(n_tiles, 1, TOP_K * tm)
    tok = pl.BlockSpec((tm, d), lambda i: (i, 0))
    return pl.pallas_call(
        _combine_kernel,
        grid=(n_tiles,),
        in_specs=[
            pl.BlockSpec((1, 1, TOP_K * tm), lambda i: (i, 0, 0), memory_space=pltpu.SMEM),
            pl.BlockSpec((1, 1, TOP_K * tm), lambda i: (jnp.minimum(i + 1, n_tiles - 1), 0, 0),
                         memory_space=pltpu.SMEM),
            pl.BlockSpec(memory_space=pl.ANY),
            pl.BlockSpec((tm, LANES), lambda i: (i, 0)),
            tok,
            pl.BlockSpec((tm, p2.shape[1]), lambda i: (i, 0)),
            _const_spec(wproj.shape),
            _const_spec(wgate.shape),
            _const_spec(vec.shape),
        ],
        out_specs=tok,
        out_shape=jax.ShapeDtypeStruct((n, d), F32),
        scratch_shapes=[pltpu.VMEM((2, TOP_K * tm // SUBLANES, SUBLANES, d), F32), pltpu.SemaphoreType.DMA((2,))],
        compiler_params=_params("arbitrary"),
        name="combine_ple",
    )(idx, idx, _rows3(y_rows), gate, x2, p2, wproj, wgate, vec)


def _routing(route, counts, tm):
    n = route.shape[0]
    experts = route[:, :TOP_K]
    pos = route[:, TOP_K:2 * TOP_K]
    counts = counts[0, :N_EXPERTS].astype(jnp.int32)
    padded = (counts + tm - 1) // tm * tm
    pad_end = jnp.cumsum(padded).astype(jnp.int32)
    pad_start = pad_end - padded
    start_of = jnp.sum(jnp.where(experts[:, :, None] == jnp.arange(N_EXPERTS, dtype=jnp.int32), pad_start, 0), axis=-1)
    slots = (pos + start_of).astype(jnp.int32)
    n_blocks = n * TOP_K // tm + N_EXPERTS
    block_e = jnp.minimum(jnp.searchsorted(pad_end, jnp.arange(n_blocks) * tm, side='right'),
                          N_EXPERTS - 1).astype(jnp.int32)
    n_active = (pad_end[-1:] // tm).astype(jnp.int32)
    return slots, block_e, n_active, jnp.where(padded > 0, pad_end, 0), n_blocks * tm


def _tile(n, want):
    t = min(n, want)
    assert n % t == 0 and t % 8 == 0, (n, t)
    return t


def kernel(x, p, rwkv_mix, rwkv_w_rkv, rwkv_w0, rwkv_w1, rwkv_w2, rwkv_a0, rwkv_a1, rwkv_a2, rwkv_v0, rwkv_v1, rwkv_v2, rwkv_g1, rwkv_g2, rwkv_k_k, rwkv_k_a, rwkv_r_k, rwkv_lnx_g, rwkv_lnx_b, rwkv_w_o, conv_w_in, conv_w, conv_w_out, ln_mix_g, ln_mix_b, router_w, router_b, moe_w_gu, moe_b_gu, moe_w_down, moe_b_down, ln_ffn_g, ln_ffn_b, ple_w_proj, ple_w_gate, ple_b_gate, ln_ple_g, ln_ple_b):
    bsz, t, d = x.shape
    n = bsz * t
    tm_exp = 512
    assert d % LANES == 0 and t % WKV_CHUNK == 0 and (n * TOP_K) % tm_exp == 0
    tm_seq = _tile(t, 256)
    tm_tok = _tile(n, 256)
    head = jnp.arange(d, dtype=jnp.int32) // HEAD_SIZE
    bd = (head[:, None] == head[None, :]).astype(BF16)
    bf = lambda z: z.astype(BF16)
    zeros = jnp.zeros((d,), F32)
    router_wp = jnp.zeros((DEPTH, d, LANES), F32).at[:, :, :N_EXPERTS].set(router_w)
    router_bp = jnp.zeros((DEPTH, 1, LANES), F32).at[:, 0, :N_EXPERTS].set(router_b)

    v_first = None
    for i in range(DEPTH):
        j = i // 2
        if i % 2 == 0:
            has_vlora = j > 0
            vec = jnp.stack([rwkv_w0[j], rwkv_a0[j], rwkv_v0[j - 1] if has_vlora else zeros, rwkv_k_k[j],
                             rwkv_k_a[j], zeros, zeros, zeros])
            r, logw, k, v, a, b, g = _rwkv_pre(
                x, v_first, rwkv_mix[j], vec, bf(rwkv_w_rkv[j]), bf(rwkv_w1[j]), bf(rwkv_w2[j]), bf(rwkv_a1[j]),
                bf(rwkv_a2[j]), bf(rwkv_g1[j]), bf(rwkv_g2[j]),
                bf(rwkv_v1[j - 1]) if has_vlora else None, bf(rwkv_v2[j - 1]) if has_vlora else None, bd, tm_seq)
            if not has_vlora:
                v_first = v
            y = _wkv(r, logw, k, v, a, b)
            vec = jnp.stack([rwkv_r_k[j].reshape(d), rwkv_lnx_g[j], rwkv_lnx_b[j], ln_mix_g[i], ln_mix_b[i],
                             zeros, zeros, zeros])
            flat = lambda z: z.reshape(n, d)
            x2 = _rwkv_post(flat(y), flat(r), flat(k), flat(v), flat(g), flat(x), vec, bf(rwkv_w_o[j]), bd, tm_tok)
        else:
            vec = jnp.stack([ln_mix_g[i], ln_mix_b[i]] + [zeros] * 6)
            x2 = _conv_mix(x, bf(conv_w_in[j]), conv_w[j], bf(conv_w_out[j]), vec, tm_seq).reshape(n, d)
        route, gate, counts = _router(x2, router_wp[i], router_bp[i], tm_tok)
        slots, block_e, n_active, pad_end, n_rows = _routing(route, counts, tm_exp)
        xs = _dispatch(x2, slots, pad_end, n_rows, tm_tok, tm_exp)
        y_rows = _experts(xs, block_e, n_active, bf(moe_w_gu[i]), moe_b_gu[i], bf(moe_w_down[i]), moe_b_down[i],
                          tm_exp)
        vec = jnp.stack([ln_ffn_g[i], ln_ffn_b[i], ple_b_gate[i], ln_ple_g[i], ln_ple_b[i], zeros, zeros, zeros])
        x = _combine(y_rows, slots, gate, x2, p[i].reshape(n, -1), bf(ple_w_proj[i]), bf(ple_w_gate[i]), vec,
                     tm_tok).reshape(bsz, t, d)
    return x
```

```python
import functools

import jax
import jax.numpy as jnp
from jax import lax
from jax.experimental import pallas as pl
from jax.experimental.pallas import tpu as pltpu

DEPTH = 4
HEAD_SIZE = 64
LANES = 128
SUBLANES = 8
N_EXPERTS = 32
TOP_K = 4
GN_EPS = HEAD_SIZE * 1e-5
LN_EPS = 1e-5
SWIGLU_LIMIT = 7.0
SWIGLU_ALPHA = 1.702
DEEPNORM_ALPHA = (2.0 * DEPTH) ** 0.25
WKV_CHUNK = 64
WKV_STEP_CHUNKS = 2
VMEM_LIMIT = 56 * 1024 * 1024

F32 = jnp.float32
BF16 = jnp.bfloat16


def _dot(a, b):
    return jnp.dot(a, b, preferred_element_type=F32)


def _dot_nt(a, b):
    return lax.dot_general(a, b, (((1,), (1,)), ((), ())), preferred_element_type=F32)


def _dot_tn(a, b):
    return lax.dot_general(a, b, (((0,), (0,)), ((), ())), preferred_element_type=F32)


def _bdot(a, b):
    return _dot(a.astype(BF16), b)


def _headsum(z, bd, split=False):
    hi = z.astype(BF16)
    if not split:
        return _dot(hi, bd)
    lo = (z - hi.astype(F32)).astype(BF16)
    return _dot(hi, bd) + _dot(lo, bd)


def _layer_norm(x, g, b):
    mu = jnp.mean(x, axis=-1, keepdims=True)
    d = x - mu
    var = jnp.mean(d * d, axis=-1, keepdims=True)
    return d * lax.rsqrt(var + LN_EPS) * g + b


def _sigmoid(x):
    return 1.0 / (1.0 + jnp.exp(-x))


def _softplus(x):
    return jnp.maximum(x, 0.0) + jnp.log(1.0 + jnp.exp(-jnp.abs(x)))


def _shift_rows(x, prev_rows, n):
    out = pltpu.roll(x, n, axis=0)
    row = lax.broadcasted_iota(jnp.int32, x.shape, 0)
    p = prev_rows.shape[0]
    for i in range(n):
        out = jnp.where(row == i, prev_rows[p - n + i:p - n + i + 1, :], out)
    return out


def _params(*sem):
    return pltpu.CompilerParams(dimension_semantics=sem, vmem_limit_bytes=VMEM_LIMIT)


def _const_spec(shape):
    nd = len(shape)
    return pl.BlockSpec(shape, lambda *_: (0,) * nd)


def _to_row_tiles(ref, x):
    for s in range(SUBLANES):
        ref[pl.ds(s, x.shape[0], stride=SUBLANES), :] = x[:, s * LANES:(s + 1) * LANES]


def _from_row_tiles(ref, first_row, rows):
    return jnp.concatenate([ref[pl.ds(first_row * SUBLANES + s, rows, stride=SUBLANES), :] for s in range(SUBLANES)],
                           axis=1)


def _tile_of(ref, row):
    return ref.at[pl.ds(pl.multiple_of(row * SUBLANES, SUBLANES), SUBLANES), :]


def _rwkv_pre_kernel(*refs, has_vlora):
    if has_vlora:
        (x_ref, xp_ref, vf_ref, mix_ref, vec_ref, wrkv_ref, w1_ref, w2_ref, a1_ref, a2_ref, g1_ref, g2_ref,
         v1_ref, v2_ref, bd_ref, r_out, w_out, k_out, v_out, a_out, b_out, g_out) = refs
    else:
        (x_ref, xp_ref, mix_ref, vec_ref, wrkv_ref, w1_ref, w2_ref, a1_ref, a2_ref, g1_ref, g2_ref,
         bd_ref, r_out, w_out, k_out, v_out, a_out, b_out, g_out) = refs
    t = pl.program_id(1)
    x = x_ref[0]
    prev = jnp.where(t == 0, 0.0, xp_ref[0])
    xx = _shift_rows(x, prev, 1) - x
    mix = mix_ref[...]
    vec = vec_ref[...]
    w0, a0, v0, k_k, k_a = (vec[i:i + 1, :] for i in range(5))

    def mixed(i):
        return (x + xx * mix[i:i + 1, :]).astype(BF16)

    xv = mixed(2)
    r = _dot(mixed(0), wrkv_ref[0])
    k = _dot(mixed(1), wrkv_ref[1])
    v = _dot(xv, wrkv_ref[2])
    ww = w0 + _bdot(jnp.tanh(_dot(mixed(3), w1_ref[...])), w2_ref[...])
    logw = -jnp.exp(-_softplus(-ww) - 0.5)
    a = _sigmoid(a0 + _bdot(_dot(mixed(4), a1_ref[...]), a2_ref[...]))
    g = _bdot(_sigmoid(_dot(mixed(5), g1_ref[...])), g2_ref[...])
    if has_vlora:
        v = v + (vf_ref[0] - v) * _sigmoid(v0 + _bdot(_dot(xv, v1_ref[...]), v2_ref[...]))
    kk = k * k_k
    kk = kk / jnp.maximum(jnp.sqrt(_headsum(kk * kk, bd_ref[...])), 1e-12)
    r_out[0] = r
    w_out[0] = logw
    k_out[0] = k * (1.0 + (a - 1.0) * k_a)
    v_out[0] = v
    a_out[0] = -kk
    b_out[0] = kk * a
    g_out[0] = g


def _rwkv_pre(x, v_first, mix, vec, wrkv, w1, w2, a1, a2, g1, g2, v1, v2, bd, tm):
    bsz, t, d = x.shape
    has_vlora = v_first is not None
    tok = pl.BlockSpec((1, tm, d), lambda b, i: (b, i, 0))
    prev = pl.BlockSpec((1, 8, d), lambda b, i: (b, jnp.maximum(i * (tm // 8) - 1, 0), 0))
    ins = [x, x] + ([v_first] if has_vlora else []) + [mix, vec, wrkv, w1, w2, a1, a2, g1, g2]
    specs = [tok, prev] + ([tok] if has_vlora else []) + [_const_spec(z.shape) for z in
                                                          (mix, vec, wrkv, w1, w2, a1, a2, g1, g2)]
    if has_vlora:
        ins += [v1, v2]
        specs += [_const_spec(v1.shape), _const_spec(v2.shape)]
    ins.append(bd)
    specs.append(_const_spec(bd.shape))
    out = jax.ShapeDtypeStruct((bsz, t, d), F32)
    return pl.pallas_call(
        functools.partial(_rwkv_pre_kernel, has_vlora=has_vlora),
        grid=(bsz, t // tm),
        in_specs=specs,
        out_specs=[tok] * 7,
        out_shape=[out] * 7,
        compiler_params=_params("arbitrary", "arbitrary"),
        name="rwkv_pre",
    )(*ins)


def _wkv_kernel(r_ref, w_ref, k_ref, v_ref, a_ref, b_ref, y_ref, s_ref):
    n_pairs = s_ref.shape[0]
    L = WKV_CHUNK
    n_chunks = w_ref.shape[1] // L
    pairs = range(n_chunks * n_pairs)
    n = 2 * L

    @pl.when(pl.program_id(1) == 0)
    def _():
        s_ref[...] = jnp.zeros_like(s_ref)

    row = lax.broadcasted_iota(jnp.int32, (n, n), 0)
    col = lax.broadcasted_iota(jnp.int32, (n, n), 1)
    strict = row > col
    incl = row >= col
    eye = (row == col).astype(F32)
    tri = (lax.broadcasted_iota(jnp.int32, (L, L), 0) >= lax.broadcasted_iota(jnp.int32, (L, L), 1)).astype(F32)
    first_head = lax.broadcasted_iota(jnp.int32, (L, LANES), 1) < HEAD_SIZE

    def stack(z):
        return jnp.concatenate([jnp.where(first_head, z, 0.0), jnp.where(first_head, 0.0, z)], axis=0)

    def lanes(ref, p):
        ci, hp = divmod(p, n_pairs)
        return ref[0, ci * L:(ci + 1) * L, hp * LANES:(hp + 1) * LANES]

    logw = [lanes(w_ref, p) for p in pairs]
    c = [jnp.dot(tri, lw, precision=lax.Precision.HIGHEST, preferred_element_type=F32) for lw in logw]
    g_inc = [jnp.exp(ci) for ci in c]
    g_inv = [jnp.exp(-ci) for ci in c]
    a2 = [stack(lanes(a_ref, p) * jnp.exp(c[p] - logw[p])) for p in pairs]
    r2 = [stack(lanes(r_ref, p) * g_inc[p]) for p in pairs]
    b2 = [stack(lanes(b_ref, p) * g_inv[p]).astype(BF16) for p in pairs]
    k2 = [stack(lanes(k_ref, p) * g_inv[p]).astype(BF16) for p in pairs]
    v2 = [stack(lanes(v_ref, p)).astype(BF16) for p in pairs]
    gram = [_dot_nt(jnp.concatenate([a2[p], r2[p]], axis=0).astype(BF16), jnp.concatenate([b2[p], k2[p]], axis=0))
            for p in pairs]
    a_ab = [jnp.where(strict, gm[:n, :n], 0.0) for gm in gram]
    a_ak = [jnp.where(strict, gm[:n, n:], 0.0) for gm in gram]
    a_rb = [jnp.where(incl, gm[n:, :n], 0.0).astype(BF16) for gm in gram]
    a_rk = [jnp.where(incl, gm[n:, n:], 0.0).astype(BF16) for gm in gram]
    x1 = [_bdot(a_ak[p], v2[p]) for p in pairs]
    inv = [eye + m for m in a_ab]
    q = a_ab
    for _ in range(L.bit_length() - 2):
        qb = [m.astype(BF16) for m in q]
        q = [_dot(m, m) for m in qb]
        inv = [inv[p] + _bdot(inv[p], q[p].astype(BF16)) for p in pairs]
    wu = [_bdot(inv[p], jnp.concatenate([a2[p], x1[p]], axis=1).astype(BF16)).astype(BF16) for p in pairs]
    zt = [_dot_tn(wu[p], b2[p]) for p in pairs]
    vk = [_dot_tn(v2[p], k2[p]) for p in pairs]
    qy = [_dot(a_rb[p], wu[p]) for p in pairs]
    yv2 = [qy[p][:, LANES:] + _dot(a_rk[p], v2[p]) for p in pairs]
    for hp in range(n_pairs):
        s = s_ref[hp]
        for ci in range(n_chunks):
            p = ci * n_pairs + hp
            q2 = r2[p] + qy[p][:, :LANES]
            q_c = q2[:L] + q2[L:]
            y_v = yv2[p][:L] + yv2[p][L:]
            sb = s.astype(BF16)
            y_ref[0, ci * L:(ci + 1) * L, hp * LANES:(hp + 1) * LANES] = _dot_nt(q_c.astype(BF16), sb) + y_v
            s = (s + _bdot(sb, zt[p][:LANES].astype(BF16)) + zt[p][LANES:] + vk[p]) * g_inc[p][L - 1:L, :]
        s_ref[hp] = s


def _wkv(r, logw, k, v, a, b):
    bsz, t, d = r.shape
    step = WKV_CHUNK * WKV_STEP_CHUNKS if t % (WKV_CHUNK * WKV_STEP_CHUNKS) == 0 else WKV_CHUNK
    blk = pl.BlockSpec((1, step, d), lambda i, c: (i, c, 0))
    return pl.pallas_call(
        _wkv_kernel,
        grid=(bsz, t // step),
        in_specs=[blk] * 6,
        out_specs=blk,
        out_shape=jax.ShapeDtypeStruct((bsz, t, d), F32),
        scratch_shapes=[pltpu.VMEM((d // LANES, LANES, LANES), F32)],
        compiler_params=_params("arbitrary", "arbitrary"),
        name="wkv_scan",
    )(r, logw, k, v, a, b)


def _rwkv_post_kernel(y_ref, r_ref, k_ref, v_ref, g_ref, x_ref, vec_ref, wo_ref, bd_ref, o_ref):
    vec = vec_ref[...]
    r_k, lnx_g, lnx_b, ln_g, ln_b = (vec[i:i + 1, :] for i in range(5))
    bd = bd_ref[...]
    y = y_ref[...]
    mu = _headsum(y, bd, split=True) * (1.0 / HEAD_SIZE)
    dy = y - mu
    var = _headsum(dy * dy, bd) * (1.0 / HEAD_SIZE)
    yn = dy * lax.rsqrt(var + GN_EPS) * lnx_g + lnx_b
    bonus = _headsum(r_ref[...] * k_ref[...] * r_k, bd) * v_ref[...]
    mixed = _bdot((yn + bonus) * g_ref[...], wo_ref[...])
    o_ref[...] = _layer_norm(DEEPNORM_ALPHA * x_ref[...] + mixed, ln_g, ln_b)


def _rwkv_post(y, r, k, v, g, x2, vec, wo, bd, tm):
    n, d = x2.shape
    tok = pl.BlockSpec((tm, d), lambda i: (i, 0))
    return pl.pallas_call(
        _rwkv_post_kernel,
        grid=(n // tm,),
        in_specs=[tok] * 6 + [_const_spec(vec.shape), _const_spec(wo.shape), _const_spec(bd.shape)],
        out_specs=tok,
        out_shape=jax.ShapeDtypeStruct((n, d), F32),
        compiler_params=_params("arbitrary"),
        name="rwkv_post",
    )(y, r, k, v, g, x2, vec, wo, bd)


def _conv_kernel(x_ref, xp_ref, win_ref, cw_ref, wout_ref, vec_ref, o_ref):
    d = x_ref.shape[-1]
    t = pl.program_id(1)
    x = x_ref[0]
    proj = _bdot(x, win_ref[...])
    gate_b = proj[:, :d]
    ch = proj[:, d:2 * d] * proj[:, 2 * d:]
    pproj = _bdot(xp_ref[0], win_ref[:, d:])
    ch_prev = jnp.where(t == 0, 0.0, pproj[:, :d] * pproj[:, d:])
    cw = cw_ref[...]
    u = cw[0:1, :] * _shift_rows(ch, ch_prev, 2) + cw[1:2, :] * _shift_rows(ch, ch_prev, 1) + cw[2:3, :] * ch
    mixed = _bdot(gate_b * u, wout_ref[...])
    vec = vec_ref[...]
    o_ref[0] = _layer_norm(DEEPNORM_ALPHA * x + mixed, vec[0:1, :], vec[1:2, :])


def _conv_mix(x, win, cw, wout, vec, tm):
    bsz, t, d = x.shape
    tok = pl.BlockSpec((1, tm, d), lambda b, i: (b, i, 0))
    prev = pl.BlockSpec((1, 8, d), lambda b, i: (b, jnp.maximum(i * (tm // 8) - 1, 0), 0))
    return pl.pallas_call(
        _conv_kernel,
        grid=(bsz, t // tm),
        in_specs=[tok, prev] + [_const_spec(z.shape) for z in (win, cw, wout, vec)],
        out_specs=tok,
        out_shape=jax.ShapeDtypeStruct((bsz, t, d), F32),
        compiler_params=_params("arbitrary", "arbitrary"),
        name="conv_mix",
    )(x, x, win, cw, wout, vec)


def _router_kernel(x_ref, w_ref, b_ref, route_ref, gate_ref, cnt_ref, base_ref):
    @pl.when(pl.program_id(0) == 0)
    def _():
        base_ref[...] = jnp.zeros_like(base_ref)

    logits = jnp.dot(x_ref[...], w_ref[...], precision=lax.Precision.HIGHEST, preferred_element_type=F32) + b_ref[...]
    tm = logits.shape[0]
    lane = lax.broadcasted_iota(jnp.int32, logits.shape, 1)
    logits = jnp.where(lane < N_EXPERTS, logits, -jnp.inf)
    vals, idxs = [], []
    for _ in range(TOP_K):
        m = jnp.max(logits, axis=-1, keepdims=True)
        sel = jnp.min(jnp.where(logits == m, lane, LANES), axis=-1, keepdims=True)
        vals.append(m)
        idxs.append(sel)
        logits = jnp.where(lane == sel, -jnp.inf, logits)
    es = [jnp.exp(m - vals[0]) for m in vals]
    tot = es[0] + es[1] + es[2] + es[3]
    chosen = [lane == sel for sel in idxs]
    onehot = jnp.zeros(logits.shape, F32)
    for ch in chosen:
        onehot = onehot + ch.astype(F32)
    before = (lax.broadcasted_iota(jnp.int32, (tm, tm), 0) > lax.broadcasted_iota(jnp.int32, (tm, tm), 1))
    pos_all = base_ref[0:1, :] + _dot(before.astype(BF16), onehot.astype(BF16))
    route = jnp.zeros(logits.shape, jnp.int32)
    gate_out = jnp.zeros(logits.shape, F32)
    for i in range(TOP_K):
        pos = jnp.sum(jnp.where(chosen[i], pos_all, 0.0), axis=-1, keepdims=True).astype(jnp.int32)
        route = jnp.where(lane == i, idxs[i], route)
        route = jnp.where(lane == TOP_K + i, pos, route)
        gate_out = jnp.where(lane == i, es[i] / tot, gate_out)
    route_ref[...] = route
    gate_ref[...] = gate_out
    total = base_ref[...] + jnp.sum(onehot, axis=0, keepdims=True)
    base_ref[...] = total
    cnt_ref[...] = total


def _router(x2, w, b, tm):
    n, d = x2.shape
    tok = pl.BlockSpec((tm, d), lambda i: (i, 0))
    out = pl.BlockSpec((tm, LANES), lambda i: (i, 0))
    cnt = pl.BlockSpec((SUBLANES, LANES), lambda i: (0, 0))
    return pl.pallas_call(
        _router_kernel,
        grid=(n // tm,),
        in_specs=[tok, _const_spec(w.shape), _const_spec(b.shape)],
        out_specs=[out, out, cnt],
        out_shape=[jax.ShapeDtypeStruct((n, LANES), jnp.int32), jax.ShapeDtypeStruct((n, LANES), F32),
                   jax.ShapeDtypeStruct((SUBLANES, LANES), F32)],
        scratch_shapes=[pltpu.VMEM((SUBLANES, LANES), F32)],
        compiler_params=_params("arbitrary"),
        name="router",
    )(x2, w, b)


def _dispatch_kernel(pe_ref, slot_ref, x_ref, xs_hbm, stage0, stage1, zero_ref, sem, zsem, *, tile_rows, n_steps):
    tm = x_ref.shape[0]
    i = pl.program_id(0)

    def zero_copy(e):
        first = pl.multiple_of((pe_ref[e] - tile_rows) * SUBLANES, SUBLANES)
        return pltpu.make_async_copy(zero_ref, xs_hbm.at[pl.ds(first, tile_rows * SUBLANES), :], zsem)

    @pl.when(i == 0)
    def _():
        zero_ref[...] = jnp.zeros_like(zero_ref)
        for e in range(N_EXPERTS):
            @pl.when(pe_ref[e] > 0)
            def _():
                zero_copy(e).start()
        for e in range(N_EXPERTS):
            @pl.when(pe_ref[e] > 0)
            def _():
                zero_copy(e).wait()

    def row_copy(stage, parity, j, slot):
        return pltpu.make_async_copy(_tile_of(stage, j), _tile_of(xs_hbm, slot), sem.at[parity])

    def wait_all(stage, parity):
        def body(grp, carry):
            for _ in range(SUBLANES * TOP_K):
                row_copy(stage, parity, 0, 0).wait()
            return carry

        lax.fori_loop(0, tm // SUBLANES, body, 0)

    def step(stage, parity):
        @pl.when(i >= 2)
        def _():
            wait_all(stage, parity)

        _to_row_tiles(stage, x_ref[...])

        def body(grp, carry):
            for u in range(SUBLANES):
                for k in range(TOP_K):
                    j = grp * SUBLANES + u
                    row_copy(stage, parity, j, slot_ref[0, 0, j * TOP_K + k]).start(priority=k % 2)
            return carry

        lax.fori_loop(0, tm // SUBLANES, body, 0)

    @pl.when(i % 2 == 0)
    def _():
        step(stage0, 0)

    @pl.when(i % 2 == 1)
    def _():
        step(stage1, 1)

    @pl.when(i == n_steps - 1)
    def _():
        wait_all(stage0, 0)
        if n_steps >= 2:
            wait_all(stage1, 1)


def _dispatch(x2, slots, pad_end, n_rows, tm, tm_exp):
    n, d = x2.shape
    n_tiles = n // tm
    grid_spec = pltpu.PrefetchScalarGridSpec(
        num_scalar_prefetch=1,
        grid=(n_tiles,),
        in_specs=[
            pl.BlockSpec((1, 1, TOP_K * tm), lambda i, pe: (i, 0, 0), memory_space=pltpu.SMEM),
            pl.BlockSpec((tm, d), lambda i, pe: (i, 0)),
        ],
        out_specs=pl.BlockSpec(memory_space=pl.ANY),
        scratch_shapes=[pltpu.VMEM((tm * SUBLANES, LANES), F32), pltpu.VMEM((tm * SUBLANES, LANES), F32),
                        pltpu.VMEM((tm_exp * SUBLANES, LANES), F32), pltpu.SemaphoreType.DMA((2,)),
                        pltpu.SemaphoreType.DMA(())],
    )
    return pl.pallas_call(
        functools.partial(_dispatch_kernel, tile_rows=tm_exp, n_steps=n_tiles),
        grid_spec=grid_spec,
        out_shape=jax.ShapeDtypeStruct((n_rows * SUBLANES, LANES), F32),
        compiler_params=_params("arbitrary"),
        name="dispatch",
    )(pad_end, slots.reshape(n_tiles, 1, TOP_K * tm), x2)


def _expert_kernel(be_ref, nact_ref, x_ref, wgu_ref, bgu_ref, wd_ref, bd_ref, o_ref):
    f = wd_ref.shape[1]
    tm = x_ref.shape[0] // SUBLANES
    active = pl.program_id(0) < nact_ref[0]

    @pl.when(active)
    def _():
        hcat = _bdot(_from_row_tiles(x_ref, 0, tm), wgu_ref[0]) + bgu_ref[0]
        glu = jnp.minimum(hcat[:, :f], SWIGLU_LIMIT)
        lin = jnp.clip(hcat[:, f:], -SWIGLU_LIMIT, SWIGLU_LIMIT)
        act = glu * _sigmoid(SWIGLU_ALPHA * glu) * (lin + 1.0)
        _to_row_tiles(o_ref, _bdot(act, wd_ref[0]) + bd_ref[0])

    @pl.when(jnp.logical_not(active))
    def _():
        o_ref[...] = jnp.zeros_like(o_ref)


def _experts(xs, block_e, n_active, layer, wgu, bgu, wd, bdn, tm):
    d = wgu.shape[2]
    n_blocks = xs.shape[0] // (tm * SUBLANES)
    f2 = wgu.shape[-1]
    f = wd.shape[2]
    nl = wgu.shape[0]
    wgu, wd = wgu.reshape(nl * N_EXPERTS, d, f2), wd.reshape(nl * N_EXPERTS, f, d)
    bgu, bdn = bgu.reshape(nl * N_EXPERTS, 1, f2), bdn.reshape(nl * N_EXPERTS, 1, d)
    first = layer * N_EXPERTS
    grid_spec = pltpu.PrefetchScalarGridSpec(
        num_scalar_prefetch=2,
        grid=(n_blocks,),
        in_specs=[
            pl.BlockSpec((tm * SUBLANES, LANES), lambda i, be, na: (jnp.minimum(i, na[0] - 1), 0)),
            pl.BlockSpec((1, d, f2), lambda i, be, na: (first + be[i], 0, 0)),
            pl.BlockSpec((1, 1, f2), lambda i, be, na: (first + be[i], 0, 0)),
            pl.BlockSpec((1, f, d), lambda i, be, na: (first + be[i], 0, 0)),
            pl.BlockSpec((1, 1, d), lambda i, be, na: (first + be[i], 0, 0)),
        ],
        out_specs=pl.BlockSpec((tm * SUBLANES, LANES), lambda i, be, na: (i, 0)),
    )
    return pl.pallas_call(
        _expert_kernel,
        grid_spec=grid_spec,
        out_shape=jax.ShapeDtypeStruct(xs.shape, F32),
        compiler_params=_params("arbitrary"),
        name="experts",
    )(block_e, n_active, xs, wgu, bgu, wd, bdn)


def _combine_kernel(idx_ref, idxn_ref, y_hbm, gate_ref, x_ref, p_ref, wproj_ref, wgate_ref, vec_ref, o_ref, buf, sem):
    tm = x_ref.shape[0]
    i = pl.program_id(0)
    slot = i % 2

    def row_copy(src_row, buf_slot, j):
        return pltpu.make_async_copy(_tile_of(y_hbm, src_row), _tile_of(buf.at[buf_slot], j), sem.at[buf_slot])

    def start_rows(ids_ref, buf_slot):
        def body(grp, carry):
            for u in range(SUBLANES):
                for k in range(TOP_K):
                    j = grp * SUBLANES + u
                    row_copy(ids_ref[0, 0, j * TOP_K + k], buf_slot, k * tm + j).start(priority=k % 2)
            return carry

        lax.fori_loop(0, tm // SUBLANES, body, 0)

    @pl.when(i == 0)
    def _():
        start_rows(idx_ref, 0)

    @pl.when(i + 1 < pl.num_programs(0))
    def _():
        start_rows(idxn_ref, 1 - slot)

    def wait_body(grp, carry):
        for _ in range(SUBLANES * TOP_K):
            row_copy(0, slot, 0).wait()
        return carry

    lax.fori_loop(0, tm // SUBLANES, wait_body, 0)
    vec = vec_ref[...]
    ffn_g, ffn_b, b_gate, ple_g, ple_b = (vec[j:j + 1, :] for j in range(5))
    gate = gate_ref[...]
    rows = buf.at[slot]
    ffn = None
    for k in range(TOP_K):
        part = _from_row_tiles(rows, k * tm, tm) * gate[:, k:k + 1]
        ffn = part if ffn is None else ffn + part
    x = _layer_norm(DEEPNORM_ALPHA * x_ref[...] + ffn, ffn_g, ffn_b)
    ple = _bdot(p_ref[...], wproj_ref[...]) * _sigmoid(_bdot(x, wgate_ref[...]) + b_gate)
    o_ref[...] = _layer_norm(DEEPNORM_ALPHA * x + ple, ple_g, ple_b)


def _combine(y_rows, slots, gate, x2, layer, p2, wproj, wgate, vec, tm):
    n, d = x2.shape
    n_tiles = n // tm
    p_first = layer * n_tiles
    idx = slots.reshape(n_tiles, 1, TOP_K * tm)
    tok = pl.BlockSpec((tm, d), lambda i: (i, 0))
    return pl.pallas_call(
        _combine_kernel,
        grid=(n_tiles,),
        in_specs=[
            pl.BlockSpec((1, 1, TOP_K * tm), lambda i: (i, 0, 0), memory_space=pltpu.SMEM),
            pl.BlockSpec((1, 1, TOP_K * tm), lambda i: (jnp.minimum(i + 1, n_tiles - 1), 0, 0),
                         memory_space=pltpu.SMEM),
            pl.BlockSpec(memory_space=pl.ANY),
            pl.BlockSpec((tm, LANES), lambda i: (i, 0)),
            tok,
            pl.BlockSpec((tm, p2.shape[1]), lambda i: (p_first + i, 0)),
            _const_spec(wproj.shape),
            _const_spec(wgate.shape),
            _const_spec(vec.shape),
        ],
        out_specs=tok,
        out_shape=jax.ShapeDtypeStruct((n, d), F32),
        scratch_shapes=[pltpu.VMEM((2, TOP_K * tm * SUBLANES, LANES), F32), pltpu.SemaphoreType.DMA((2,))],
        compiler_params=_params("arbitrary"),
        name="combine_ple",
    )(idx, idx, y_rows, gate, x2, p2, wproj, wgate, vec)


def _routing(route, counts, tm):
    n = route.shape[0]
    experts = route[:, :TOP_K]
    pos = route[:, TOP_K:2 * TOP_K]
    counts = counts[0, :N_EXPERTS].astype(jnp.int32)
    padded = (counts + tm - 1) // tm * tm
    pad_end = jnp.cumsum(padded).astype(jnp.int32)
    pad_start = pad_end - padded
    start_of = jnp.sum(jnp.where(experts[:, :, None] == jnp.arange(N_EXPERTS, dtype=jnp.int32), pad_start, 0), axis=-1)
    slots = (pos + start_of).astype(jnp.int32)
    n_blocks = n * TOP_K // tm + N_EXPERTS
    block_start = jnp.arange(n_blocks, dtype=jnp.int32) * tm
    block_e = jnp.minimum(jnp.sum(pad_end[None, :] <= block_start[:, None], axis=1), N_EXPERTS - 1).astype(jnp.int32)
    n_active = (pad_end[-1:] // tm).astype(jnp.int32)
    return slots, block_e, n_active, jnp.where(padded > 0, pad_end, 0), n_blocks * tm


def _tile(n, want):
    t = min(n, want)
    assert n % t == 0 and t % 8 == 0, (n, t)
    return t


def kernel(x, p, rwkv_mix, rwkv_w_rkv, rwkv_w0, rwkv_w1, rwkv_w2, rwkv_a0, rwkv_a1, rwkv_a2, rwkv_v0, rwkv_v1, rwkv_v2, rwkv_g1, rwkv_g2, rwkv_k_k, rwkv_k_a, rwkv_r_k, rwkv_lnx_g, rwkv_lnx_b, rwkv_w_o, conv_w_in, conv_w, conv_w_out, ln_mix_g, ln_mix_b, router_w, router_b, moe_w_gu, moe_b_gu, moe_w_down, moe_b_down, ln_ffn_g, ln_ffn_b, ple_w_proj, ple_w_gate, ple_b_gate, ln_ple_g, ln_ple_b):
    bsz, t, d = x.shape
    n = bsz * t
    tm_exp = 512
    assert d == SUBLANES * LANES and t % WKV_CHUNK == 0 and (n * TOP_K) % tm_exp == 0
    tm_seq = _tile(t, 256)
    tm_tok = _tile(n, 256)
    head = jnp.arange(d, dtype=jnp.int32) // HEAD_SIZE
    bd = (head[:, None] == head[None, :]).astype(BF16)
    bf = lambda z: z.astype(BF16)
    zeros = jnp.zeros((d,), F32)
    router_wp = jnp.zeros((DEPTH, d, LANES), F32).at[:, :, :N_EXPERTS].set(router_w)
    router_bp = jnp.zeros((DEPTH, 1, LANES), F32).at[:, 0, :N_EXPERTS].set(router_b)
    w_gu_bf, w_down_bf = bf(moe_w_gu), bf(moe_w_down)
    p_all = p.reshape(DEPTH * n, p.shape[-1])

    v_first = None
    for i in range(DEPTH):
        j = i // 2
        if i % 2 == 0:
            has_vlora = j > 0
            vec = jnp.stack([rwkv_w0[j], rwkv_a0[j], rwkv_v0[j - 1] if has_vlora else zeros, rwkv_k_k[j],
                             rwkv_k_a[j], zeros, zeros, zeros])
            r, logw, k, v, a, b, g = _rwkv_pre(
                x, v_first, rwkv_mix[j], vec, bf(rwkv_w_rkv[j]), bf(rwkv_w1[j]), bf(rwkv_w2[j]), bf(rwkv_a1[j]),
                bf(rwkv_a2[j]), bf(rwkv_g1[j]), bf(rwkv_g2[j]),
                bf(rwkv_v1[j - 1]) if has_vlora else None, bf(rwkv_v2[j - 1]) if has_vlora else None, bd, tm_seq)
            if not has_vlora:
                v_first = v
            y = _wkv(r, logw, k, v, a, b)
            vec = jnp.stack([rwkv_r_k[j].reshape(d), rwkv_lnx_g[j], rwkv_lnx_b[j], ln_mix_g[i], ln_mix_b[i],
                             zeros, zeros, zeros])
            flat = lambda z: z.reshape(n, d)
            x2 = _rwkv_post(flat(y), flat(r), flat(k), flat(v), flat(g), flat(x), vec, bf(rwkv_w_o[j]), bd, tm_tok)
        else:
            vec = jnp.stack([ln_mix_g[i], ln_mix_b[i]] + [zeros] * 6)
            x2 = _conv_mix(x, bf(conv_w_in[j]), conv_w[j], bf(conv_w_out[j]), vec, tm_seq).reshape(n, d)
        route, gate, counts = _router(x2, router_wp[i], router_bp[i], tm_tok)
        slots, block_e, n_active, pad_end, n_rows = _routing(route, counts, tm_exp)
        xs = _dispatch(x2, slots, pad_end, n_rows, tm_tok, tm_exp)
        y_rows = _experts(xs, block_e, n_active, i, w_gu_bf, moe_b_gu, w_down_bf, moe_b_down, tm_exp)
        vec = jnp.stack([ln_ffn_g[i], ln_ffn_b[i], ple_b_gate[i], ln_ple_g[i], ln_ple_b[i], zeros, zeros, zeros])
        x = _combine(y_rows, slots, gate, x2, i, p_all, bf(ple_w_proj[i]), bf(ple_w_gate[i]), vec,
                     tm_tok).reshape(bsz, t, d)
    return x
```

```python
import functools

import jax
import jax.numpy as jnp
from jax import lax
from jax.experimental import pallas as pl
from jax.experimental.pallas import tpu as pltpu

DEPTH = 4
HEAD_SIZE = 64
LANES = 128
SUBLANES = 8
N_EXPERTS = 32
TOP_K = 4
GN_EPS = HEAD_SIZE * 1e-5
LN_EPS = 1e-5
SWIGLU_LIMIT = 7.0
SWIGLU_ALPHA = 1.702
DEEPNORM_ALPHA = (2.0 * DEPTH) ** 0.25
WKV_CHUNK = 64
WKV_STEP_CHUNKS = 2
VMEM_LIMIT = 56 * 1024 * 1024

F32 = jnp.float32
BF16 = jnp.bfloat16


def _dot(a, b):
    return jnp.dot(a, b, preferred_element_type=F32)


def _dot_nt(a, b):
    return lax.dot_general(a, b, (((1,), (1,)), ((), ())), preferred_element_type=F32)


def _dot_tn(a, b):
    return lax.dot_general(a, b, (((0,), (0,)), ((), ())), preferred_element_type=F32)


def _bdot(a, b):
    return _dot(a.astype(BF16), b)


def _headsum(z, bd, split=False):
    hi = z.astype(BF16)
    if not split:
        return _dot(hi, bd)
    lo = (z - hi.astype(F32)).astype(BF16)
    return _dot(hi, bd) + _dot(lo, bd)


def _layer_norm(x, g, b):
    mu = jnp.mean(x, axis=-1, keepdims=True)
    d = x - mu
    var = jnp.mean(d * d, axis=-1, keepdims=True)
    return d * lax.rsqrt(var + LN_EPS) * g + b


def _sigmoid(x):
    return 1.0 / (1.0 + jnp.exp(-x))


def _softplus(x):
    return jnp.maximum(x, 0.0) + jnp.log(1.0 + jnp.exp(-jnp.abs(x)))


def _shift_rows(x, prev_rows, n):
    out = pltpu.roll(x, n, axis=0)
    row = lax.broadcasted_iota(jnp.int32, x.shape, 0)
    p = prev_rows.shape[0]
    for i in range(n):
        out = jnp.where(row == i, prev_rows[p - n + i:p - n + i + 1, :], out)
    return out


def _params(*sem):
    return pltpu.CompilerParams(dimension_semantics=sem, vmem_limit_bytes=VMEM_LIMIT)


def _const_spec(shape):
    nd = len(shape)
    return pl.BlockSpec(shape, lambda *_: (0,) * nd)


def _to_row_tiles(ref, x):
    for s in range(SUBLANES):
        ref[pl.ds(s, x.shape[0], stride=SUBLANES), :] = x[:, s * LANES:(s + 1) * LANES]


def _from_row_tiles(ref, first_row, rows):
    return jnp.concatenate([ref[pl.ds(first_row * SUBLANES + s, rows, stride=SUBLANES), :] for s in range(SUBLANES)],
                           axis=1)


def _tile_of(ref, row):
    return ref.at[pl.ds(pl.multiple_of(row * SUBLANES, SUBLANES), SUBLANES), :]


def _rwkv_pre_kernel(*refs, has_vlora):
    if has_vlora:
        (x_ref, xp_ref, vf_ref, mix_ref, vec_ref, wrkv_ref, w1_ref, w2_ref, a1_ref, a2_ref, g1_ref, g2_ref,
         v1_ref, v2_ref, bd_ref, r_out, w_out, k_out, v_out, a_out, b_out, g_out) = refs
    else:
        (x_ref, xp_ref, mix_ref, vec_ref, wrkv_ref, w1_ref, w2_ref, a1_ref, a2_ref, g1_ref, g2_ref,
         bd_ref, r_out, w_out, k_out, v_out, a_out, b_out, g_out) = refs
    t = pl.program_id(1)
    x = x_ref[0]
    prev = jnp.where(t == 0, 0.0, xp_ref[0])
    xx = _shift_rows(x, prev, 1) - x
    mix = mix_ref[...]
    vec = vec_ref[...]
    w0, a0, v0, k_k, k_a = (vec[i:i + 1, :] for i in range(5))

    def mixed(i):
        return (x + xx * mix[i:i + 1, :]).astype(BF16)

    xv = mixed(2)
    r = _dot(mixed(0), wrkv_ref[0])
    k = _dot(mixed(1), wrkv_ref[1])
    v = _dot(xv, wrkv_ref[2])
    ww = w0 + _bdot(jnp.tanh(_dot(mixed(3), w1_ref[...])), w2_ref[...])
    logw = -jnp.exp(-_softplus(-ww) - 0.5)
    a = _sigmoid(a0 + _bdot(_dot(mixed(4), a1_ref[...]), a2_ref[...]))
    g = _bdot(_sigmoid(_dot(mixed(5), g1_ref[...])), g2_ref[...])
    if has_vlora:
        v = v + (vf_ref[0] - v) * _sigmoid(v0 + _bdot(_dot(xv, v1_ref[...]), v2_ref[...]))
    kk = k * k_k
    kk = kk / jnp.maximum(jnp.sqrt(_headsum(kk * kk, bd_ref[...])), 1e-12)
    r_out[0] = r
    w_out[0] = logw
    k_out[0] = k * (1.0 + (a - 1.0) * k_a)
    v_out[0] = v
    a_out[0] = -kk
    b_out[0] = kk * a
    g_out[0] = g


def _rwkv_pre(x, v_first, mix, vec, wrkv, w1, w2, a1, a2, g1, g2, v1, v2, bd, tm):
    bsz, t, d = x.shape
    has_vlora = v_first is not None
    tok = pl.BlockSpec((1, tm, d), lambda b, i: (b, i, 0))
    prev = pl.BlockSpec((1, 8, d), lambda b, i: (b, jnp.maximum(i * (tm // 8) - 1, 0), 0))
    ins = [x, x] + ([v_first] if has_vlora else []) + [mix, vec, wrkv, w1, w2, a1, a2, g1, g2]
    specs = [tok, prev] + ([tok] if has_vlora else []) + [_const_spec(z.shape) for z in
                                                          (mix, vec, wrkv, w1, w2, a1, a2, g1, g2)]
    if has_vlora:
        ins += [v1, v2]
        specs += [_const_spec(v1.shape), _const_spec(v2.shape)]
    ins.append(bd)
    specs.append(_const_spec(bd.shape))
    out = jax.ShapeDtypeStruct((bsz, t, d), F32)
    return pl.pallas_call(
        functools.partial(_rwkv_pre_kernel, has_vlora=has_vlora),
        grid=(bsz, t // tm),
        in_specs=specs,
        out_specs=[tok] * 7,
        out_shape=[out] * 7,
        compiler_params=_params("arbitrary", "arbitrary"),
        name="rwkv_pre",
    )(*ins)


def _wkv_kernel(r_ref, w_ref, k_ref, v_ref, a_ref, b_ref, y_ref, s_ref):
    n_pairs = s_ref.shape[0]
    L = WKV_CHUNK
    n_chunks = w_ref.shape[1] // L
    pairs = range(n_chunks * n_pairs)
    n = 2 * L

    @pl.when(pl.program_id(1) == 0)
    def _():
        s_ref[...] = jnp.zeros_like(s_ref)

    row = lax.broadcasted_iota(jnp.int32, (n, n), 0)
    col = lax.broadcasted_iota(jnp.int32, (n, n), 1)
    strict = row > col
    incl = row >= col
    eye = (row == col).astype(F32)
    tri = (lax.broadcasted_iota(jnp.int32, (L, L), 0) >= lax.broadcasted_iota(jnp.int32, (L, L), 1)).astype(BF16)
    first_head = lax.broadcasted_iota(jnp.int32, (L, LANES), 1) < HEAD_SIZE

    def stack(z):
        return jnp.concatenate([jnp.where(first_head, z, 0.0), jnp.where(first_head, 0.0, z)], axis=0)

    def lanes(ref, p):
        ci, hp = divmod(p, n_pairs)
        return ref[0, ci * L:(ci + 1) * L, hp * LANES:(hp + 1) * LANES]

    c_all = []
    for ci in range(n_chunks):
        lw = w_ref[0, ci * L:(ci + 1) * L, :]
        hi = lw.astype(BF16)
        lo = (lw - hi.astype(F32)).astype(BF16)
        c_all.append(_dot(tri, hi) + _dot(tri, lo))
    logw = [lanes(w_ref, p) for p in pairs]
    c = [c_all[p // n_pairs][:, (p % n_pairs) * LANES:(p % n_pairs + 1) * LANES] for p in pairs]
    g_inc = [jnp.exp(ci) for ci in c]
    g_inv = [jnp.exp(-ci) for ci in c]
    a2 = [stack(lanes(a_ref, p) * jnp.exp(c[p] - logw[p])) for p in pairs]
    r2 = [stack(lanes(r_ref, p) * g_inc[p]) for p in pairs]
    b2 = [stack(lanes(b_ref, p) * g_inv[p]).astype(BF16) for p in pairs]
    k2 = [stack(lanes(k_ref, p) * g_inv[p]).astype(BF16) for p in pairs]
    v2 = [stack(lanes(v_ref, p)).astype(BF16) for p in pairs]
    gram = [_dot_nt(jnp.concatenate([a2[p], r2[p]], axis=0).astype(BF16), jnp.concatenate([b2[p], k2[p]], axis=0))
            for p in pairs]
    a_ab = [jnp.where(strict, gm[:n, :n], 0.0) for gm in gram]
    a_ak = [jnp.where(strict, gm[:n, n:], 0.0) for gm in gram]
    a_rb = [jnp.where(incl, gm[n:, :n], 0.0).astype(BF16) for gm in gram]
    a_rk = [jnp.where(incl, gm[n:, n:], 0.0).astype(BF16) for gm in gram]
    x1 = [_bdot(a_ak[p], v2[p]) for p in pairs]
    inv = [eye + m for m in a_ab]
    q = a_ab
    for _ in range(L.bit_length() - 2):
        qb = [m.astype(BF16) for m in q]
        q = [_dot(m, m) for m in qb]
        inv = [inv[p] + _bdot(inv[p], q[p].astype(BF16)) for p in pairs]
    wu = [_bdot(inv[p], jnp.concatenate([a2[p], x1[p]], axis=1).astype(BF16)).astype(BF16) for p in pairs]
    zt = [_dot_tn(wu[p], b2[p]) for p in pairs]
    vk = [_dot_tn(v2[p], k2[p]) for p in pairs]
    qy = [_dot(a_rb[p], wu[p]) for p in pairs]
    yv2 = [qy[p][:, LANES:] + _dot(a_rk[p], v2[p]) for p in pairs]
    for hp in range(n_pairs):
        s = s_ref[hp]
        for ci in range(n_chunks):
            p = ci * n_pairs + hp
            q2 = r2[p] + qy[p][:, :LANES]
            q_c = q2[:L] + q2[L:]
            y_v = yv2[p][:L] + yv2[p][L:]
            sb = s.astype(BF16)
            y_ref[0, ci * L:(ci + 1) * L, hp * LANES:(hp + 1) * LANES] = _dot_nt(q_c.astype(BF16), sb) + y_v
            s = (s + _bdot(sb, zt[p][:LANES].astype(BF16)) + zt[p][LANES:] + vk[p]) * g_inc[p][L - 1:L, :]
        s_ref[hp] = s


def _wkv(r, logw, k, v, a, b):
    bsz, t, d = r.shape
    step = WKV_CHUNK * WKV_STEP_CHUNKS if t % (WKV_CHUNK * WKV_STEP_CHUNKS) == 0 else WKV_CHUNK
    blk = pl.BlockSpec((1, step, d), lambda i, c: (i, c, 0))
    return pl.pallas_call(
        _wkv_kernel,
        grid=(bsz, t // step),
        in_specs=[blk] * 6,
        out_specs=blk,
        out_shape=jax.ShapeDtypeStruct((bsz, t, d), F32),
        scratch_shapes=[pltpu.VMEM((d // LANES, LANES, LANES), F32)],
        compiler_params=_params("arbitrary", "arbitrary"),
        name="wkv_scan",
    )(r, logw, k, v, a, b)


def _rwkv_post_kernel(y_ref, r_ref, k_ref, v_ref, g_ref, x_ref, vec_ref, wo_ref, bd_ref, o_ref):
    vec = vec_ref[...]
    r_k, lnx_g, lnx_b, ln_g, ln_b = (vec[i:i + 1, :] for i in range(5))
    bd = bd_ref[...]
    y = y_ref[...]
    mu = _headsum(y, bd, split=True) * (1.0 / HEAD_SIZE)
    dy = y - mu
    var = _headsum(dy * dy, bd) * (1.0 / HEAD_SIZE)
    yn = dy * lax.rsqrt(var + GN_EPS) * lnx_g + lnx_b
    bonus = _headsum(r_ref[...] * k_ref[...] * r_k, bd) * v_ref[...]
    mixed = _bdot((yn + bonus) * g_ref[...], wo_ref[...])
    o_ref[...] = _layer_norm(DEEPNORM_ALPHA * x_ref[...] + mixed, ln_g, ln_b)


def _rwkv_post(y, r, k, v, g, x2, vec, wo, bd, tm):
    n, d = x2.shape
    tok = pl.BlockSpec((tm, d), lambda i: (i, 0))
    return pl.pallas_call(
        _rwkv_post_kernel,
        grid=(n // tm,),
        in_specs=[tok] * 6 + [_const_spec(vec.shape), _const_spec(wo.shape), _const_spec(bd.shape)],
        out_specs=tok,
        out_shape=jax.ShapeDtypeStruct((n, d), F32),
        compiler_params=_params("arbitrary"),
        name="rwkv_post",
    )(y, r, k, v, g, x2, vec, wo, bd)


def _conv_kernel(x_ref, xp_ref, win_ref, cw_ref, wout_ref, vec_ref, o_ref):
    d = x_ref.shape[-1]
    t = pl.program_id(1)
    x = x_ref[0]
    proj = _bdot(x, win_ref[...])
    gate_b = proj[:, :d]
    ch = proj[:, d:2 * d] * proj[:, 2 * d:]
    pproj = _bdot(xp_ref[0], win_ref[:, d:])
    ch_prev = jnp.where(t == 0, 0.0, pproj[:, :d] * pproj[:, d:])
    cw = cw_ref[...]
    u = cw[0:1, :] * _shift_rows(ch, ch_prev, 2) + cw[1:2, :] * _shift_rows(ch, ch_prev, 1) + cw[2:3, :] * ch
    mixed = _bdot(gate_b * u, wout_ref[...])
    vec = vec_ref[...]
    o_ref[0] = _layer_norm(DEEPNORM_ALPHA * x + mixed, vec[0:1, :], vec[1:2, :])


def _conv_mix(x, win, cw, wout, vec, tm):
    bsz, t, d = x.shape
    tok = pl.BlockSpec((1, tm, d), lambda b, i: (b, i, 0))
    prev = pl.BlockSpec((1, 8, d), lambda b, i: (b, jnp.maximum(i * (tm // 8) - 1, 0), 0))
    return pl.pallas_call(
        _conv_kernel,
        grid=(bsz, t // tm),
        in_specs=[tok, prev] + [_const_spec(z.shape) for z in (win, cw, wout, vec)],
        out_specs=tok,
        out_shape=jax.ShapeDtypeStruct((bsz, t, d), F32),
        compiler_params=_params("arbitrary", "arbitrary"),
        name="conv_mix",
    )(x, x, win, cw, wout, vec)


def _router_kernel(x_ref, w_ref, b_ref, route_ref, gate_ref, cnt_ref, base_ref):
    @pl.when(pl.program_id(0) == 0)
    def _():
        base_ref[...] = jnp.zeros_like(base_ref)

    logits = jnp.dot(x_ref[...], w_ref[...], precision=lax.Precision.HIGHEST, preferred_element_type=F32) + b_ref[...]
    tm = logits.shape[0]
    lane = lax.broadcasted_iota(jnp.int32, logits.shape, 1)
    logits = jnp.where(lane < N_EXPERTS, logits, -jnp.inf)
    vals, idxs = [], []
    for _ in range(TOP_K):
        m = jnp.max(logits, axis=-1, keepdims=True)
        sel = jnp.min(jnp.where(logits == m, lane, LANES), axis=-1, keepdims=True)
        vals.append(m)
        idxs.append(sel)
        logits = jnp.where(lane == sel, -jnp.inf, logits)
    es = [jnp.exp(m - vals[0]) for m in vals]
    tot = es[0] + es[1] + es[2] + es[3]
    chosen = [lane == sel for sel in idxs]
    onehot = jnp.zeros(logits.shape, F32)
    for ch in chosen:
        onehot = onehot + ch.astype(F32)
    before = (lax.broadcasted_iota(jnp.int32, (tm, tm), 0) > lax.broadcasted_iota(jnp.int32, (tm, tm), 1))
    pos_all = base_ref[0:1, :] + _dot(before.astype(BF16), onehot.astype(BF16))
    route = jnp.zeros(logits.shape, jnp.int32)
    gate_out = jnp.zeros(logits.shape, F32)
    for i in range(TOP_K):
        pos = jnp.sum(jnp.where(chosen[i], pos_all, 0.0), axis=-1, keepdims=True).astype(jnp.int32)
        route = jnp.where(lane == i, idxs[i], route)
        route = jnp.where(lane == TOP_K + i, pos, route)
        gate_out = jnp.where(lane == i, es[i] / tot, gate_out)
    route_ref[...] = route
    gate_ref[...] = gate_out
    total = base_ref[...] + jnp.sum(onehot, axis=0, keepdims=True)
    base_ref[...] = total
    cnt_ref[...] = total


def _router(x2, w, b, tm):
    n, d = x2.shape
    tok = pl.BlockSpec((tm, d), lambda i: (i, 0))
    out = pl.BlockSpec((tm, LANES), lambda i: (i, 0))
    cnt = pl.BlockSpec((SUBLANES, LANES), lambda i: (0, 0))
    return pl.pallas_call(
        _router_kernel,
        grid=(n // tm,),
        in_specs=[tok, _const_spec(w.shape), _const_spec(b.shape)],
        out_specs=[out, out, cnt],
        out_shape=[jax.ShapeDtypeStruct((n, LANES), jnp.int32), jax.ShapeDtypeStruct((n, LANES), F32),
                   jax.ShapeDtypeStruct((SUBLANES, LANES), F32)],
        scratch_shapes=[pltpu.VMEM((SUBLANES, LANES), F32)],
        compiler_params=_params("arbitrary"),
        name="router",
    )(x2, w, b)


def _dispatch_kernel(pe_ref, slot_ref, x_ref, xs_hbm, stage0, stage1, zero_ref, sem, zsem, *, tile_rows, n_steps):
    tm = x_ref.shape[0]
    i = pl.program_id(0)

    def zero_copy(e):
        first = pl.multiple_of((pe_ref[e] - tile_rows) * SUBLANES, SUBLANES)
        return pltpu.make_async_copy(zero_ref, xs_hbm.at[pl.ds(first, tile_rows * SUBLANES), :], zsem)

    @pl.when(i == 0)
    def _():
        zero_ref[...] = jnp.zeros_like(zero_ref)
        for e in range(N_EXPERTS):
            @pl.when(pe_ref[e] > 0)
            def _():
                zero_copy(e).start()
        for e in range(N_EXPERTS):
            @pl.when(pe_ref[e] > 0)
            def _():
                zero_copy(e).wait()

    def row_copy(stage, parity, j, slot):
        return pltpu.make_async_copy(_tile_of(stage, j), _tile_of(xs_hbm, slot), sem.at[parity])

    def wait_all(stage, parity):
        def body(grp, carry):
            for _ in range(SUBLANES * TOP_K):
                row_copy(stage, parity, 0, 0).wait()
            return carry

        lax.fori_loop(0, tm // SUBLANES, body, 0)

    def step(stage, parity):
        @pl.when(i >= 2)
        def _():
            wait_all(stage, parity)

        _to_row_tiles(stage, x_ref[...])

        def body(grp, carry):
            for u in range(SUBLANES):
                for k in range(TOP_K):
                    j = grp * SUBLANES + u
                    row_copy(stage, parity, j, slot_ref[0, 0, j * TOP_K + k]).start(priority=k % 2)
            return carry

        lax.fori_loop(0, tm // SUBLANES, body, 0)

    @pl.when(i % 2 == 0)
    def _():
        step(stage0, 0)

    @pl.when(i % 2 == 1)
    def _():
        step(stage1, 1)

    @pl.when(i == n_steps - 1)
    def _():
        wait_all(stage0, 0)
        if n_steps >= 2:
            wait_all(stage1, 1)


def _dispatch(x2, slots, pad_end, n_rows, tm, tm_exp):
    n, d = x2.shape
    n_tiles = n // tm
    grid_spec = pltpu.PrefetchScalarGridSpec(
        num_scalar_prefetch=1,
        grid=(n_tiles,),
        in_specs=[
            pl.BlockSpec((1, 1, TOP_K * tm), lambda i, pe: (i, 0, 0), memory_space=pltpu.SMEM),
            pl.BlockSpec((tm, d), lambda i, pe: (i, 0)),
        ],
        out_specs=pl.BlockSpec(memory_space=pl.ANY),
        scratch_shapes=[pltpu.VMEM((tm * SUBLANES, LANES), F32), pltpu.VMEM((tm * SUBLANES, LANES), F32),
                        pltpu.VMEM((tm_exp * SUBLANES, LANES), F32), pltpu.SemaphoreType.DMA((2,)),
                        pltpu.SemaphoreType.DMA(())],
    )
    return pl.pallas_call(
        functools.partial(_dispatch_kernel, tile_rows=tm_exp, n_steps=n_tiles),
        grid_spec=grid_spec,
        out_shape=jax.ShapeDtypeStruct((n_rows * SUBLANES, LANES), F32),
        compiler_params=_params("arbitrary"),
        name="dispatch",
    )(pad_end, slots.reshape(n_tiles, 1, TOP_K * tm), x2)


def _expert_kernel(be_ref, nact_ref, x_ref, wgu_ref, bgu_ref, wd_ref, bd_ref, o_ref, wgu_bf, wd_bf):
    f = wd_ref.shape[1]
    tm = x_ref.shape[0] // SUBLANES
    i = pl.program_id(0)
    active = i < nact_ref[0]
    new_expert = jnp.logical_or(i == 0, be_ref[i] != be_ref[jnp.maximum(i - 1, 0)])

    @pl.when(jnp.logical_and(active, new_expert))
    def _():
        wgu_bf[...] = wgu_ref[0].astype(BF16)
        wd_bf[...] = wd_ref[0].astype(BF16)

    @pl.when(active)
    def _():
        hcat = _bdot(_from_row_tiles(x_ref, 0, tm), wgu_bf[...]) + bgu_ref[0]
        glu = jnp.minimum(hcat[:, :f], SWIGLU_LIMIT)
        lin = jnp.clip(hcat[:, f:], -SWIGLU_LIMIT, SWIGLU_LIMIT)
        act = glu * _sigmoid(SWIGLU_ALPHA * glu) * (lin + 1.0)
        _to_row_tiles(o_ref, _bdot(act, wd_bf[...]) + bd_ref[0])

    @pl.when(jnp.logical_not(active))
    def _():
        o_ref[...] = jnp.zeros_like(o_ref)


def _experts(xs, block_e, n_active, layer, wgu, bgu, wd, bdn, tm):
    d = wgu.shape[2]
    n_blocks = xs.shape[0] // (tm * SUBLANES)
    f2 = wgu.shape[-1]
    f = wd.shape[2]
    nl = wgu.shape[0]
    wgu, wd = wgu.reshape(nl * N_EXPERTS, d, f2), wd.reshape(nl * N_EXPERTS, f, d)
    bgu, bdn = bgu.reshape(nl * N_EXPERTS, 1, f2), bdn.reshape(nl * N_EXPERTS, 1, d)
    first = layer * N_EXPERTS
    grid_spec = pltpu.PrefetchScalarGridSpec(
        num_scalar_prefetch=2,
        grid=(n_blocks,),
        in_specs=[
            pl.BlockSpec((tm * SUBLANES, LANES), lambda i, be, na: (jnp.minimum(i, na[0] - 1), 0)),
            pl.BlockSpec((1, d, f2), lambda i, be, na: (first + be[i], 0, 0)),
            pl.BlockSpec((1, 1, f2), lambda i, be, na: (first + be[i], 0, 0)),
            pl.BlockSpec((1, f, d), lambda i, be, na: (first + be[i], 0, 0)),
            pl.BlockSpec((1, 1, d), lambda i, be, na: (first + be[i], 0, 0)),
        ],
        out_specs=pl.BlockSpec((tm * SUBLANES, LANES), lambda i, be, na: (i, 0)),
        scratch_shapes=[pltpu.VMEM((d, f2), BF16), pltpu.VMEM((f, d), BF16)],
    )
    return pl.pallas_call(
        _expert_kernel,
        grid_spec=grid_spec,
        out_shape=jax.ShapeDtypeStruct(xs.shape, F32),
        compiler_params=_params("arbitrary"),
        name="experts",
    )(block_e, n_active, xs, wgu, bgu, wd, bdn)


def _combine_kernel(idx_ref, idxn_ref, y_hbm, gate_ref, x_ref, p_ref, wproj_ref, wgate_ref, vec_ref, o_ref,
                    buf_a, buf_b, sem, *, n_steps):
    tm = x_ref.shape[0] // 2
    i = pl.program_id(0)
    vec = vec_ref[...]
    ffn_g, ffn_b, b_gate, ple_g, ple_b = (vec[j:j + 1, :] for j in range(5))

    def row_copy(src_row, buf, which, j):
        return pltpu.make_async_copy(_tile_of(y_hbm, src_row), _tile_of(buf, j), sem.at[which])

    def start_rows(ids_ref, first, buf, which):
        for j in range(tm):
            for k in range(TOP_K):
                row_copy(ids_ref[0, 0, (first + j) * TOP_K + k], buf, which, k * tm + j).start(priority=k % 2)

    def wait_rows(buf, which):
        def body(grp, carry):
            for _ in range(SUBLANES * TOP_K):
                row_copy(0, buf, which, 0).wait()
            return carry

        lax.fori_loop(0, tm // SUBLANES, body, 0)

    def compute(buf, first):
        tok = pl.ds(first, tm)
        gate = gate_ref[tok, :]
        ffn = None
        for k in range(TOP_K):
            part = _from_row_tiles(buf, k * tm, tm) * gate[:, k:k + 1]
            ffn = part if ffn is None else ffn + part
        x = _layer_norm(DEEPNORM_ALPHA * x_ref[tok, :] + ffn, ffn_g, ffn_b)
        ple = _bdot(p_ref[tok, :], wproj_ref[...]) * _sigmoid(_bdot(x, wgate_ref[...]) + b_gate)
        o_ref[tok, :] = _layer_norm(DEEPNORM_ALPHA * x + ple, ple_g, ple_b)

    @pl.when(i == 0)
    def _():
        start_rows(idx_ref, 0, buf_a, 0)

    wait_rows(buf_a, 0)
    start_rows(idx_ref, tm, buf_b, 1)
    compute(buf_a, 0)
    wait_rows(buf_b, 1)
    start_rows(idxn_ref, 0, buf_a, 0)
    compute(buf_b, tm)

    @pl.when(i == n_steps - 1)
    def _():
        wait_rows(buf_a, 0)


def _combine(y_rows, slots, gate, x2, layer, p2, wproj, wgate, vec, tm):
    n, d = x2.shape
    n_steps = n // (2 * tm)
    p_first = layer * n_steps
    idx = slots.reshape(n_steps, 1, 2 * TOP_K * tm)
    tok = pl.BlockSpec((2 * tm, d), lambda i: (i, 0))
    buf = pltpu.VMEM((TOP_K * tm * SUBLANES, LANES), F32)
    return pl.pallas_call(
        functools.partial(_combine_kernel, n_steps=n_steps),
        grid=(n_steps,),
        in_specs=[
            pl.BlockSpec((1, 1, 2 * TOP_K * tm), lambda i: (i, 0, 0), memory_space=pltpu.SMEM),
            pl.BlockSpec((1, 1, 2 * TOP_K * tm), lambda i: (jnp.minimum(i + 1, n_steps - 1), 0, 0),
                         memory_space=pltpu.SMEM),
            pl.BlockSpec(memory_space=pl.ANY),
            pl.BlockSpec((2 * tm, LANES), lambda i: (i, 0)),
            tok,
            pl.BlockSpec((2 * tm, p2.shape[1]), lambda i: (p_first + i, 0)),
            _const_spec(wproj.shape),
            _const_spec(wgate.shape),
            _const_spec(vec.shape),
        ],
        out_specs=tok,
        out_shape=jax.ShapeDtypeStruct((n, d), F32),
        scratch_shapes=[buf, buf, pltpu.SemaphoreType.DMA((2,))],
        compiler_params=_params("arbitrary"),
        name="combine_ple",
    )(idx, idx, y_rows, gate, x2, p2, wproj, wgate, vec)


def _routing(route, counts, tm):
    n = route.shape[0]
    experts = route[:, :TOP_K]
    pos = route[:, TOP_K:2 * TOP_K]
    counts = counts[0, :N_EXPERTS].astype(jnp.int32)
    padded = (counts + tm - 1) // tm * tm
    pad_end = jnp.cumsum(padded).astype(jnp.int32)
    pad_start = pad_end - padded
    start_of = jnp.sum(jnp.where(experts[:, :, None] == jnp.arange(N_EXPERTS, dtype=jnp.int32), pad_start, 0), axis=-1)
    slots = (pos + start_of).astype(jnp.int32)
    n_blocks = n * TOP_K // tm + N_EXPERTS
    block_start = jnp.arange(n_blocks, dtype=jnp.int32) * tm
    block_e = jnp.minimum(jnp.sum(pad_end[None, :] <= block_start[:, None], axis=1), N_EXPERTS - 1).astype(jnp.int32)
    n_active = (pad_end[-1:] // tm).astype(jnp.int32)
    return slots, block_e, n_active, jnp.where(padded > 0, pad_end, 0), n_blocks * tm


def _tile(n, want):
    t = min(n, want)
    assert n % t == 0 and t % 8 == 0, (n, t)
    return t


def kernel(x, p, rwkv_mix, rwkv_w_rkv, rwkv_w0, rwkv_w1, rwkv_w2, rwkv_a0, rwkv_a1, rwkv_a2, rwkv_v0, rwkv_v1, rwkv_v2, rwkv_g1, rwkv_g2, rwkv_k_k, rwkv_k_a, rwkv_r_k, rwkv_lnx_g, rwkv_lnx_b, rwkv_w_o, conv_w_in, conv_w, conv_w_out, ln_mix_g, ln_mix_b, router_w, router_b, moe_w_gu, moe_b_gu, moe_w_down, moe_b_down, ln_ffn_g, ln_ffn_b, ple_w_proj, ple_w_gate, ple_b_gate, ln_ple_g, ln_ple_b):
    bsz, t, d = x.shape
    n = bsz * t
    tm_exp = 512
    assert d == SUBLANES * LANES and t % WKV_CHUNK == 0 and (n * TOP_K) % tm_exp == 0
    tm_seq = _tile(t, 256)
    tm_tok = _tile(n, 256)
    tm_big = _tile(t, 512)
    tm_cmb = _tile(n // 2, 256)
    head = jnp.arange(d, dtype=jnp.int32) // HEAD_SIZE
    bd = (head[:, None] == head[None, :]).astype(BF16)
    bf = lambda z: z.astype(BF16)
    zeros = jnp.zeros((d,), F32)
    router_wp = jnp.zeros((DEPTH, d, LANES), F32).at[:, :, :N_EXPERTS].set(router_w)
    router_bp = jnp.zeros((DEPTH, 1, LANES), F32).at[:, 0, :N_EXPERTS].set(router_b)
    p_all = p.reshape(DEPTH * n, p.shape[-1])

    v_first = None
    for i in range(DEPTH):
        j = i // 2
        if i % 2 == 0:
            has_vlora = j > 0
            vec = jnp.stack([rwkv_w0[j], rwkv_a0[j], rwkv_v0[j - 1] if has_vlora else zeros, rwkv_k_k[j],
                             rwkv_k_a[j], zeros, zeros, zeros])
            r, logw, k, v, a, b, g = _rwkv_pre(
                x, v_first, rwkv_mix[j], vec, bf(rwkv_w_rkv[j]), bf(rwkv_w1[j]), bf(rwkv_w2[j]), bf(rwkv_a1[j]),
                bf(rwkv_a2[j]), bf(rwkv_g1[j]), bf(rwkv_g2[j]),
                bf(rwkv_v1[j - 1]) if has_vlora else None, bf(rwkv_v2[j - 1]) if has_vlora else None, bd, tm_seq)
            if not has_vlora:
                v_first = v
            y = _wkv(r, logw, k, v, a, b)
            vec = jnp.stack([rwkv_r_k[j].reshape(d), rwkv_lnx_g[j], rwkv_lnx_b[j], ln_mix_g[i], ln_mix_b[i],
                             zeros, zeros, zeros])
            flat = lambda z: z.reshape(n, d)
            x2 = _rwkv_post(flat(y), flat(r), flat(k), flat(v), flat(g), flat(x), vec, bf(rwkv_w_o[j]), bd, tm_big)
        else:
            vec = jnp.stack([ln_mix_g[i], ln_mix_b[i]] + [zeros] * 6)
            x2 = _conv_mix(x, bf(conv_w_in[j]), conv_w[j], bf(conv_w_out[j]), vec, tm_big).reshape(n, d)
        route, gate, counts = _router(x2, router_wp[i], router_bp[i], tm_tok)
        slots, block_e, n_active, pad_end, n_rows = _routing(route, counts, tm_exp)
        xs = _dispatch(x2, slots, pad_end, n_rows, tm_tok, tm_exp)
        y_rows = _experts(xs, block_e, n_active, i, moe_w_gu, moe_b_gu, moe_w_down, moe_b_down, tm_exp)
        vec = jnp.stack([ln_ffn_g[i], ln_ffn_b[i], ple_b_gate[i], ln_ple_g[i], ln_ple_b[i], zeros, zeros, zeros])
        x = _combine(y_rows, slots, gate, x2, i, p_all, bf(ple_w_proj[i]), bf(ple_w_gate[i]), vec,
                     tm_cmb).reshape(bsz, t, d)
    return x
```

```python
import functools

import jax
import jax.numpy as jnp
from jax import lax
from jax.experimental import pallas as pl
from jax.experimental.pallas import tpu as pltpu

DEPTH = 4
HEAD_SIZE = 64
LANES = 128
SUBLANES = 8
N_EXPERTS = 32
TOP_K = 4
GN_EPS = HEAD_SIZE * 1e-5
LN_EPS = 1e-5
SWIGLU_LIMIT = 7.0
SWIGLU_ALPHA = 1.702
DEEPNORM_ALPHA = (2.0 * DEPTH) ** 0.25
WKV_CHUNK = 64
WKV_STEP_CHUNKS = 2
VMEM_LIMIT = 56 * 1024 * 1024

F32 = jnp.float32
BF16 = jnp.bfloat16


def _dot(a, b):
    return jnp.dot(a, b, preferred_element_type=F32)


def _dot_nt(a, b):
    return lax.dot_general(a, b, (((1,), (1,)), ((), ())), preferred_element_type=F32)


def _dot_tn(a, b):
    return lax.dot_general(a, b, (((0,), (0,)), ((), ())), preferred_element_type=F32)


def _bdot(a, b):
    return _dot(a.astype(BF16), b)


def _headsum(z, bd, split=False):
    hi = z.astype(BF16)
    if not split:
        return _dot(hi, bd)
    lo = (z - hi.astype(F32)).astype(BF16)
    return _dot(hi, bd) + _dot(lo, bd)


def _layer_norm(x, g, b):
    mu = jnp.mean(x, axis=-1, keepdims=True)
    d = x - mu
    var = jnp.mean(d * d, axis=-1, keepdims=True)
    return d * lax.rsqrt(var + LN_EPS) * g + b


def _sigmoid(x):
    return 1.0 / (1.0 + jnp.exp(-x))


def _softplus(x):
    return jnp.maximum(x, 0.0) + jnp.log(1.0 + jnp.exp(-jnp.abs(x)))


def _shift_rows(x, prev_rows, n):
    out = pltpu.roll(x, n, axis=0)
    row = lax.broadcasted_iota(jnp.int32, x.shape, 0)
    p = prev_rows.shape[0]
    for i in range(n):
        out = jnp.where(row == i, prev_rows[p - n + i:p - n + i + 1, :], out)
    return out


def _params(*sem):
    return pltpu.CompilerParams(dimension_semantics=sem, vmem_limit_bytes=VMEM_LIMIT)


def _const_spec(shape):
    nd = len(shape)
    return pl.BlockSpec(shape, lambda *_: (0,) * nd)


def _to_row_tiles(ref, x):
    for s in range(SUBLANES):
        ref[pl.ds(s, x.shape[0], stride=SUBLANES), :] = x[:, s * LANES:(s + 1) * LANES]


def _from_row_tiles(ref, first_row, rows):
    return jnp.concatenate([ref[pl.ds(first_row * SUBLANES + s, rows, stride=SUBLANES), :] for s in range(SUBLANES)],
                           axis=1)


def _tile_of(ref, row):
    return ref.at[pl.ds(pl.multiple_of(row * SUBLANES, SUBLANES), SUBLANES), :]


def _rwkv_pre_kernel(*refs, has_vlora):
    if has_vlora:
        (x_ref, xp_ref, vf_ref, mix_ref, vec_ref, wrkv_ref, w1_ref, w2_ref, a1_ref, a2_ref, g1_ref, g2_ref,
         v1_ref, v2_ref, bd_ref, r_out, w_out, k_out, v_out, a_out, b_out, g_out) = refs
    else:
        (x_ref, xp_ref, mix_ref, vec_ref, wrkv_ref, w1_ref, w2_ref, a1_ref, a2_ref, g1_ref, g2_ref,
         bd_ref, r_out, w_out, k_out, v_out, a_out, b_out, g_out) = refs
    t = pl.program_id(1)
    x = x_ref[0]
    prev = jnp.where(t == 0, 0.0, xp_ref[0])
    xx = _shift_rows(x, prev, 1) - x
    mix = mix_ref[...]
    vec = vec_ref[...]
    w0, a0, v0, k_k, k_a = (vec[i:i + 1, :] for i in range(5))

    def mixed(i):
        return (x + xx * mix[i:i + 1, :]).astype(BF16)

    xv = mixed(2)
    r = _dot(mixed(0), wrkv_ref[0])
    k = _dot(mixed(1), wrkv_ref[1])
    v = _dot(xv, wrkv_ref[2])
    ww = w0 + _bdot(jnp.tanh(_dot(mixed(3), w1_ref[...])), w2_ref[...])
    logw = -jnp.exp(-_softplus(-ww) - 0.5)
    a = _sigmoid(a0 + _bdot(_dot(mixed(4), a1_ref[...]), a2_ref[...]))
    g = _bdot(_sigmoid(_dot(mixed(5), g1_ref[...])), g2_ref[...])
    if has_vlora:
        v = v + (vf_ref[0] - v) * _sigmoid(v0 + _bdot(_dot(xv, v1_ref[...]), v2_ref[...]))
    kk = k * k_k
    kk = kk / jnp.maximum(jnp.sqrt(_headsum(kk * kk, bd_ref[...])), 1e-12)
    r_out[0] = r
    w_out[0] = logw
    k_out[0] = k * (1.0 + (a - 1.0) * k_a)
    v_out[0] = v
    a_out[0] = -kk
    b_out[0] = kk * a
    g_out[0] = g


def _rwkv_pre(x, v_first, mix, vec, wrkv, w1, w2, a1, a2, g1, g2, v1, v2, bd, tm):
    bsz, t, d = x.shape
    has_vlora = v_first is not None
    tok = pl.BlockSpec((1, tm, d), lambda b, i: (b, i, 0))
    prev = pl.BlockSpec((1, 8, d), lambda b, i: (b, jnp.maximum(i * (tm // 8) - 1, 0), 0))
    ins = [x, x] + ([v_first] if has_vlora else []) + [mix, vec, wrkv, w1, w2, a1, a2, g1, g2]
    specs = [tok, prev] + ([tok] if has_vlora else []) + [_const_spec(z.shape) for z in
                                                          (mix, vec, wrkv, w1, w2, a1, a2, g1, g2)]
    if has_vlora:
        ins += [v1, v2]
        specs += [_const_spec(v1.shape), _const_spec(v2.shape)]
    ins.append(bd)
    specs.append(_const_spec(bd.shape))
    out = jax.ShapeDtypeStruct((bsz, t, d), F32)
    return pl.pallas_call(
        functools.partial(_rwkv_pre_kernel, has_vlora=has_vlora),
        grid=(bsz, t // tm),
        in_specs=specs,
        out_specs=[tok] * 7,
        out_shape=[out] * 7,
        compiler_params=_params("arbitrary", "arbitrary"),
        name="rwkv_pre",
    )(*ins)


def _wkv_kernel(r_ref, w_ref, k_ref, v_ref, a_ref, b_ref, y_ref, s_ref):
    n_pairs = s_ref.shape[0]
    L = WKV_CHUNK
    n_chunks = w_ref.shape[1] // L
    pairs = range(n_chunks * n_pairs)
    n = 2 * L

    @pl.when(pl.program_id(1) == 0)
    def _():
        s_ref[...] = jnp.zeros_like(s_ref)

    row = lax.broadcasted_iota(jnp.int32, (n, n), 0)
    col = lax.broadcasted_iota(jnp.int32, (n, n), 1)
    strict = row > col
    incl = row >= col
    eye = (row == col).astype(F32)
    tri = (lax.broadcasted_iota(jnp.int32, (L, L), 0) >= lax.broadcasted_iota(jnp.int32, (L, L), 1)).astype(BF16)
    first_head = lax.broadcasted_iota(jnp.int32, (L, LANES), 1) < HEAD_SIZE

    def stack(z):
        return jnp.concatenate([jnp.where(first_head, z, 0.0), jnp.where(first_head, 0.0, z)], axis=0)

    def lanes(ref, p):
        ci, hp = divmod(p, n_pairs)
        return ref[0, ci * L:(ci + 1) * L, hp * LANES:(hp + 1) * LANES]

    c_all = []
    for ci in range(n_chunks):
        lw = w_ref[0, ci * L:(ci + 1) * L, :]
        hi = lw.astype(BF16)
        lo = (lw - hi.astype(F32)).astype(BF16)
        c_all.append(_dot(tri, hi) + _dot(tri, lo))
    logw = [lanes(w_ref, p) for p in pairs]
    c = [c_all[p // n_pairs][:, (p % n_pairs) * LANES:(p % n_pairs + 1) * LANES] for p in pairs]
    g_inc = [jnp.exp(ci) for ci in c]
    g_inv = [jnp.exp(-ci) for ci in c]
    a2 = [stack(lanes(a_ref, p) * jnp.exp(c[p] - logw[p])) for p in pairs]
    r2 = [stack(lanes(r_ref, p) * g_inc[p]) for p in pairs]
    b2 = [stack(lanes(b_ref, p) * g_inv[p]).astype(BF16) for p in pairs]
    k2 = [stack(lanes(k_ref, p) * g_inv[p]).astype(BF16) for p in pairs]
    v2 = [stack(lanes(v_ref, p)).astype(BF16) for p in pairs]
    gram = [_dot_nt(jnp.concatenate([a2[p], r2[p]], axis=0).astype(BF16), jnp.concatenate([b2[p], k2[p]], axis=0))
            for p in pairs]
    a_ab = [jnp.where(strict, gm[:n, :n], 0.0) for gm in gram]
    a_ak = [jnp.where(strict, gm[:n, n:], 0.0) for gm in gram]
    a_rb = [jnp.where(incl, gm[n:, :n], 0.0).astype(BF16) for gm in gram]
    a_rk = [jnp.where(incl, gm[n:, n:], 0.0).astype(BF16) for gm in gram]
    x1 = [_bdot(a_ak[p], v2[p]) for p in pairs]
    inv = [eye + m for m in a_ab]
    q = a_ab
    for _ in range(L.bit_length() - 2):
        qb = [m.astype(BF16) for m in q]
        q = [_dot(m, m) for m in qb]
        inv = [inv[p] + _bdot(inv[p], q[p].astype(BF16)) for p in pairs]
    wu = [_bdot(inv[p], jnp.concatenate([a2[p], x1[p]], axis=1).astype(BF16)).astype(BF16) for p in pairs]
    zt = [_dot_tn(wu[p], b2[p]) for p in pairs]
    vk = [_dot_tn(v2[p], k2[p]) for p in pairs]
    qy = [_dot(a_rb[p], wu[p]) for p in pairs]
    yv2 = [qy[p][:, LANES:] + _dot(a_rk[p], v2[p]) for p in pairs]
    for hp in range(n_pairs):
        s = s_ref[hp]
        for ci in range(n_chunks):
            p = ci * n_pairs + hp
            q2 = r2[p] + qy[p][:, :LANES]
            q_c = q2[:L] + q2[L:]
            y_v = yv2[p][:L] + yv2[p][L:]
            sb = s.astype(BF16)
            y_ref[0, ci * L:(ci + 1) * L, hp * LANES:(hp + 1) * LANES] = _dot_nt(q_c.astype(BF16), sb) + y_v
            s = (s + _bdot(sb, zt[p][:LANES].astype(BF16)) + zt[p][LANES:] + vk[p]) * g_inc[p][L - 1:L, :]
        s_ref[hp] = s


def _wkv(r, logw, k, v, a, b):
    bsz, t, d = r.shape
    step = WKV_CHUNK * WKV_STEP_CHUNKS if t % (WKV_CHUNK * WKV_STEP_CHUNKS) == 0 else WKV_CHUNK
    blk = pl.BlockSpec((1, step, d), lambda i, c: (i, c, 0))
    return pl.pallas_call(
        _wkv_kernel,
        grid=(bsz, t // step),
        in_specs=[blk] * 6,
        out_specs=blk,
        out_shape=jax.ShapeDtypeStruct((bsz, t, d), F32),
        scratch_shapes=[pltpu.VMEM((d // LANES, LANES, LANES), F32)],
        compiler_params=_params("arbitrary", "arbitrary"),
        name="wkv_scan",
    )(r, logw, k, v, a, b)


def _rwkv_post_kernel(y_ref, r_ref, k_ref, v_ref, g_ref, x_ref, vec_ref, wo_ref, bd_ref, o_ref):
    vec = vec_ref[...]
    r_k, lnx_g, lnx_b, ln_g, ln_b = (vec[i:i + 1, :] for i in range(5))
    bd = bd_ref[...]
    y = y_ref[...]
    mu = _headsum(y, bd, split=True) * (1.0 / HEAD_SIZE)
    dy = y - mu
    var = _headsum(dy * dy, bd) * (1.0 / HEAD_SIZE)
    yn = dy * lax.rsqrt(var + GN_EPS) * lnx_g + lnx_b
    bonus = _headsum(r_ref[...] * k_ref[...] * r_k, bd) * v_ref[...]
    mixed = _bdot((yn + bonus) * g_ref[...], wo_ref[...])
    o_ref[...] = _layer_norm(DEEPNORM_ALPHA * x_ref[...] + mixed, ln_g, ln_b)


def _rwkv_post(y, r, k, v, g, x2, vec, wo, bd, tm):
    n, d = x2.shape
    tok = pl.BlockSpec((tm, d), lambda i: (i, 0))
    return pl.pallas_call(
        _rwkv_post_kernel,
        grid=(n // tm,),
        in_specs=[tok] * 6 + [_const_spec(vec.shape), _const_spec(wo.shape), _const_spec(bd.shape)],
        out_specs=tok,
        out_shape=jax.ShapeDtypeStruct((n, d), F32),
        compiler_params=_params("arbitrary"),
        name="rwkv_post",
    )(y, r, k, v, g, x2, vec, wo, bd)


def _conv_kernel(x_ref, xp_ref, win_ref, cw_ref, wout_ref, vec_ref, o_ref):
    d = x_ref.shape[-1]
    t = pl.program_id(1)
    x = x_ref[0]
    proj = _bdot(x, win_ref[...])
    gate_b = proj[:, :d]
    ch = proj[:, d:2 * d] * proj[:, 2 * d:]
    pproj = _bdot(xp_ref[0], win_ref[:, d:])
    ch_prev = jnp.where(t == 0, 0.0, pproj[:, :d] * pproj[:, d:])
    cw = cw_ref[...]
    u = cw[0:1, :] * _shift_rows(ch, ch_prev, 2) + cw[1:2, :] * _shift_rows(ch, ch_prev, 1) + cw[2:3, :] * ch
    mixed = _bdot(gate_b * u, wout_ref[...])
    vec = vec_ref[...]
    o_ref[0] = _layer_norm(DEEPNORM_ALPHA * x + mixed, vec[0:1, :], vec[1:2, :])


def _conv_mix(x, win, cw, wout, vec, tm):
    bsz, t, d = x.shape
    tok = pl.BlockSpec((1, tm, d), lambda b, i: (b, i, 0))
    prev = pl.BlockSpec((1, 8, d), lambda b, i: (b, jnp.maximum(i * (tm // 8) - 1, 0), 0))
    return pl.pallas_call(
        _conv_kernel,
        grid=(bsz, t // tm),
        in_specs=[tok, prev] + [_const_spec(z.shape) for z in (win, cw, wout, vec)],
        out_specs=tok,
        out_shape=jax.ShapeDtypeStruct((bsz, t, d), F32),
        compiler_params=_params("arbitrary", "arbitrary"),
        name="conv_mix",
    )(x, x, win, cw, wout, vec)


def _router_kernel(x_ref, w_ref, b_ref, route_ref, gate_ref, cnt_ref, base_ref):
    @pl.when(pl.program_id(0) == 0)
    def _():
        base_ref[...] = jnp.zeros_like(base_ref)

    x = x_ref[...]
    x_hi = x.astype(BF16)
    x_lo = (x - x_hi.astype(F32)).astype(BF16)
    w = w_ref[...]
    w_hi = w.astype(BF16)
    w_lo = (w - w_hi.astype(F32)).astype(BF16)
    logits = _dot(x_hi, w_hi) + _dot(x_hi, w_lo) + _dot(x_lo, w_hi) + b_ref[...]
    tm = logits.shape[0]
    cur = jnp.transpose(logits)[:N_EXPERTS, :]
    expert = lax.broadcasted_iota(jnp.int32, cur.shape, 0)
    vals, idxs, chosen = [], [], []
    for _ in range(TOP_K):
        m = jnp.max(cur, axis=0, keepdims=True)
        sel = jnp.min(jnp.where(cur == m, expert, N_EXPERTS), axis=0, keepdims=True)
        ch = expert == sel
        vals.append(m)
        idxs.append(sel)
        chosen.append(ch)
        cur = jnp.where(ch, -jnp.inf, cur)
    es = [jnp.exp(m - vals[0]) for m in vals]
    tot = es[0] + es[1] + es[2] + es[3]
    onehot = chosen[0].astype(F32)
    for ch in chosen[1:]:
        onehot = onehot + ch.astype(F32)
    earlier = lax.broadcasted_iota(jnp.int32, (tm, tm), 0) < lax.broadcasted_iota(jnp.int32, (tm, tm), 1)
    pos_all = base_ref[:, 0:1] + _dot(onehot.astype(BF16), earlier.astype(BF16))
    pos = [jnp.sum(jnp.where(ch, pos_all, 0.0), axis=0, keepdims=True).astype(jnp.int32) for ch in chosen]
    route_ref[0] = jnp.concatenate(idxs + pos, axis=0)
    gate_ref[0] = jnp.concatenate([e / tot for e in es] + [jnp.zeros_like(tot)] * (SUBLANES - TOP_K), axis=0)
    total = base_ref[...] + jnp.sum(onehot, axis=1, keepdims=True)
    base_ref[...] = total
    cnt_ref[...] = total


def _router(x2, w, b, tm):
    n, d = x2.shape
    n_tiles = n // tm
    tok = pl.BlockSpec((tm, d), lambda i: (i, 0))
    out = pl.BlockSpec((1, SUBLANES, tm), lambda i: (i, 0, 0))
    cnt = pl.BlockSpec((N_EXPERTS, LANES), lambda i: (0, 0))
    route_t, gate_t, counts = pl.pallas_call(
        _router_kernel,
        grid=(n_tiles,),
        in_specs=[tok, _const_spec(w.shape), _const_spec(b.shape)],
        out_specs=[out, out, cnt],
        out_shape=[jax.ShapeDtypeStruct((n_tiles, SUBLANES, tm), jnp.int32),
                   jax.ShapeDtypeStruct((n_tiles, SUBLANES, tm), F32),
                   jax.ShapeDtypeStruct((N_EXPERTS, LANES), F32)],
        scratch_shapes=[pltpu.VMEM((N_EXPERTS, LANES), F32)],
        compiler_params=_params("arbitrary"),
        name="router",
    )(x2, w, b)
    route = route_t.transpose(0, 2, 1).reshape(n, SUBLANES)
    gate = jnp.pad(gate_t.transpose(0, 2, 1).reshape(n, SUBLANES), ((0, 0), (0, LANES - SUBLANES)))
    return route, gate, counts[:, 0]


def _dispatch_kernel(pe_ref, slot_ref, x_ref, xs_hbm, stage0, stage1, zero_ref, sem, zsem, *, tile_rows, n_steps):
    tm = x_ref.shape[0]
    i = pl.program_id(0)

    def zero_copy(e):
        first = pl.multiple_of((pe_ref[e] - tile_rows) * SUBLANES, SUBLANES)
        return pltpu.make_async_copy(zero_ref, xs_hbm.at[pl.ds(first, tile_rows * SUBLANES), :], zsem)

    @pl.when(i == 0)
    def _():
        zero_ref[...] = jnp.zeros_like(zero_ref)
        for e in range(N_EXPERTS):
            @pl.when(pe_ref[e] > 0)
            def _():
                zero_copy(e).start()
        for e in range(N_EXPERTS):
            @pl.when(pe_ref[e] > 0)
            def _():
                zero_copy(e).wait()

    def row_copy(stage, parity, j, slot):
        return pltpu.make_async_copy(_tile_of(stage, j), _tile_of(xs_hbm, slot), sem.at[parity])

    def wait_all(stage, parity):
        def body(grp, carry):
            for _ in range(SUBLANES * TOP_K):
                row_copy(stage, parity, 0, 0).wait()
            return carry

        lax.fori_loop(0, tm // SUBLANES, body, 0)

    def step(stage, parity):
        @pl.when(i >= 2)
        def _():
            wait_all(stage, parity)

        _to_row_tiles(stage, x_ref[...])

        def body(grp, carry):
            for u in range(SUBLANES):
                for k in range(TOP_K):
                    j = grp * SUBLANES + u
                    row_copy(stage, parity, j, slot_ref[0, 0, j * TOP_K + k]).start(priority=k % 2)
            return carry

        lax.fori_loop(0, tm // SUBLANES, body, 0)

    @pl.when(i % 2 == 0)
    def _():
        step(stage0, 0)

    @pl.when(i % 2 == 1)
    def _():
        step(stage1, 1)

    @pl.when(i == n_steps - 1)
    def _():
        wait_all(stage0, 0)
        if n_steps >= 2:
            wait_all(stage1, 1)


def _dispatch(x2, slots, pad_end, n_rows, tm, tm_exp):
    n, d = x2.shape
    n_tiles = n // tm
    grid_spec = pltpu.PrefetchScalarGridSpec(
        num_scalar_prefetch=1,
        grid=(n_tiles,),
        in_specs=[
            pl.BlockSpec((1, 1, TOP_K * tm), lambda i, pe: (i, 0, 0), memory_space=pltpu.SMEM),
            pl.BlockSpec((tm, d), lambda i, pe: (i, 0)),
        ],
        out_specs=pl.BlockSpec(memory_space=pl.ANY),
        scratch_shapes=[pltpu.VMEM((tm * SUBLANES, LANES), F32), pltpu.VMEM((tm * SUBLANES, LANES), F32),
                        pltpu.VMEM((tm_exp * SUBLANES, LANES), F32), pltpu.SemaphoreType.DMA((2,)),
                        pltpu.SemaphoreType.DMA(())],
    )
    return pl.pallas_call(
        functools.partial(_dispatch_kernel, tile_rows=tm_exp, n_steps=n_tiles),
        grid_spec=grid_spec,
        out_shape=jax.ShapeDtypeStruct((n_rows * SUBLANES, LANES), F32),
        compiler_params=_params("arbitrary"),
        name="dispatch",
    )(pad_end, slots.reshape(n_tiles, 1, TOP_K * tm), x2)


def _expert_kernel(be_ref, nact_ref, x_ref, wgu_ref, bgu_ref, wd_ref, bd_ref, o_ref, wgu_bf, wd_bf):
    f = wd_ref.shape[1]
    tm = x_ref.shape[0] // SUBLANES
    i = pl.program_id(0)
    active = i < nact_ref[0]
    new_expert = jnp.logical_or(i == 0, be_ref[i] != be_ref[jnp.maximum(i - 1, 0)])

    @pl.when(jnp.logical_and(active, new_expert))
    def _():
        wgu_bf[...] = wgu_ref[0].astype(BF16)
        wd_bf[...] = wd_ref[0].astype(BF16)

    @pl.when(active)
    def _():
        hcat = _bdot(_from_row_tiles(x_ref, 0, tm), wgu_bf[...]) + bgu_ref[0]
        glu = jnp.minimum(hcat[:, :f], SWIGLU_LIMIT)
        lin = jnp.clip(hcat[:, f:], -SWIGLU_LIMIT, SWIGLU_LIMIT)
        act = glu * _sigmoid(SWIGLU_ALPHA * glu) * (lin + 1.0)
        _to_row_tiles(o_ref, _bdot(act, wd_bf[...]) + bd_ref[0])

    @pl.when(jnp.logical_not(active))
    def _():
        o_ref[...] = jnp.zeros_like(o_ref)


def _experts(xs, block_e, n_active, layer, wgu, bgu, wd, bdn, tm):
    d = wgu.shape[2]
    n_blocks = xs.shape[0] // (tm * SUBLANES)
    f2 = wgu.shape[-1]
    f = wd.shape[2]
    nl = wgu.shape[0]
    wgu, wd = wgu.reshape(nl * N_EXPERTS, d, f2), wd.reshape(nl * N_EXPERTS, f, d)
    bgu, bdn = bgu.reshape(nl * N_EXPERTS, 1, f2), bdn.reshape(nl * N_EXPERTS, 1, d)
    first = layer * N_EXPERTS
    grid_spec = pltpu.PrefetchScalarGridSpec(
        num_scalar_prefetch=2,
        grid=(n_blocks,),
        in_specs=[
            pl.BlockSpec((tm * SUBLANES, LANES), lambda i, be, na: (jnp.minimum(i, na[0] - 1), 0)),
            pl.BlockSpec((1, d, f2), lambda i, be, na: (first + be[i], 0, 0)),
            pl.BlockSpec((1, 1, f2), lambda i, be, na: (first + be[i], 0, 0)),
            pl.BlockSpec((1, f, d), lambda i, be, na: (first + be[i], 0, 0)),
            pl.BlockSpec((1, 1, d), lambda i, be, na: (first + be[i], 0, 0)),
        ],
        out_specs=pl.BlockSpec((tm * SUBLANES, LANES), lambda i, be, na: (i, 0)),
        scratch_shapes=[pltpu.VMEM((d, f2), BF16), pltpu.VMEM((f, d), BF16)],
    )
    return pl.pallas_call(
        _expert_kernel,
        grid_spec=grid_spec,
        out_shape=jax.ShapeDtypeStruct(xs.shape, F32),
        compiler_params=_params("arbitrary"),
        name="experts",
    )(block_e, n_active, xs, wgu, bgu, wd, bdn)


def _combine_kernel(idx_ref, idxn_ref, y_hbm, gate_ref, x_ref, p_ref, wproj_ref, wgate_ref, vec_ref, o_ref,
                    buf_a, buf_b, sem, *, n_steps):
    tm = x_ref.shape[0] // 2
    i = pl.program_id(0)
    vec = vec_ref[...]
    ffn_g, ffn_b, b_gate, ple_g, ple_b = (vec[j:j + 1, :] for j in range(5))

    def row_copy(src_row, buf, which, j):
        return pltpu.make_async_copy(_tile_of(y_hbm, src_row), _tile_of(buf, j), sem.at[which])

    def start_rows(ids_ref, first, buf, which):
        for j in range(tm):
            for k in range(TOP_K):
                row_copy(ids_ref[0, 0, (first + j) * TOP_K + k], buf, which, k * tm + j).start(priority=k % 2)

    def wait_rows(buf, which):
        def body(grp, carry):
            for _ in range(SUBLANES * TOP_K):
                row_copy(0, buf, which, 0).wait()
            return carry

        lax.fori_loop(0, tm // SUBLANES, body, 0)

    def compute(buf, first):
        tok = pl.ds(first, tm)
        gate = gate_ref[tok, :]
        ffn = None
        for k in range(TOP_K):
            part = _from_row_tiles(buf, k * tm, tm) * gate[:, k:k + 1]
            ffn = part if ffn is None else ffn + part
        x = _layer_norm(DEEPNORM_ALPHA * x_ref[tok, :] + ffn, ffn_g, ffn_b)
        ple = _bdot(p_ref[tok, :], wproj_ref[...]) * _sigmoid(_bdot(x, wgate_ref[...]) + b_gate)
        o_ref[tok, :] = _layer_norm(DEEPNORM_ALPHA * x + ple, ple_g, ple_b)

    @pl.when(i == 0)
    def _():
        start_rows(idx_ref, 0, buf_a, 0)

    wait_rows(buf_a, 0)
    start_rows(idx_ref, tm, buf_b, 1)
    compute(buf_a, 0)
    wait_rows(buf_b, 1)
    start_rows(idxn_ref, 0, buf_a, 0)
    compute(buf_b, tm)

    @pl.when(i == n_steps - 1)
    def _():
        wait_rows(buf_a, 0)


def _combine(y_rows, slots, gate, x2, layer, p2, wproj, wgate, vec, tm):
    n, d = x2.shape
    n_steps = n // (2 * tm)
    p_first = layer * n_steps
    idx = slots.reshape(n_steps, 1, 2 * TOP_K * tm)
    tok = pl.BlockSpec((2 * tm, d), lambda i: (i, 0))
    buf = pltpu.VMEM((TOP_K * tm * SUBLANES, LANES), F32)
    return pl.pallas_call(
        functools.partial(_combine_kernel, n_steps=n_steps),
        grid=(n_steps,),
        in_specs=[
            pl.BlockSpec((1, 1, 2 * TOP_K * tm), lambda i: (i, 0, 0), memory_space=pltpu.SMEM),
            pl.BlockSpec((1, 1, 2 * TOP_K * tm), lambda i: (jnp.minimum(i + 1, n_steps - 1), 0, 0),
                         memory_space=pltpu.SMEM),
            pl.BlockSpec(memory_space=pl.ANY),
            pl.BlockSpec((2 * tm, LANES), lambda i: (i, 0)),
            tok,
            pl.BlockSpec((2 * tm, p2.shape[1]), lambda i: (p_first + i, 0)),
            _const_spec(wproj.shape),
            _const_spec(wgate.shape),
            _const_spec(vec.shape),
        ],
        out_specs=tok,
        out_shape=jax.ShapeDtypeStruct((n, d), F32),
        scratch_shapes=[buf, buf, pltpu.SemaphoreType.DMA((2,))],
        compiler_params=_params("arbitrary"),
        name="combine_ple",
    )(idx, idx, y_rows, gate, x2, p2, wproj, wgate, vec)


def _routing(route, counts, tm):
    n = route.shape[0]
    experts = route[:, :TOP_K]
    pos = route[:, TOP_K:2 * TOP_K]
    counts = counts.astype(jnp.int32)
    padded = (counts + tm - 1) // tm * tm
    pad_end = jnp.cumsum(padded).astype(jnp.int32)
    pad_start = pad_end - padded
    start_of = jnp.sum(jnp.where(experts[:, :, None] == jnp.arange(N_EXPERTS, dtype=jnp.int32), pad_start, 0), axis=-1)
    slots = (pos + start_of).astype(jnp.int32)
    n_blocks = n * TOP_K // tm + N_EXPERTS
    block_start = jnp.arange(n_blocks, dtype=jnp.int32) * tm
    block_e = jnp.minimum(jnp.sum(pad_end[None, :] <= block_start[:, None], axis=1), N_EXPERTS - 1).astype(jnp.int32)
    n_active = (pad_end[-1:] // tm).astype(jnp.int32)
    return slots, block_e, n_active, jnp.where(padded > 0, pad_end, 0), n_blocks * tm


def _tile(n, want):
    t = min(n, want)
    assert n % t == 0 and t % 8 == 0, (n, t)
    return t


def kernel(x, p, rwkv_mix, rwkv_w_rkv, rwkv_w0, rwkv_w1, rwkv_w2, rwkv_a0, rwkv_a1, rwkv_a2, rwkv_v0, rwkv_v1, rwkv_v2, rwkv_g1, rwkv_g2, rwkv_k_k, rwkv_k_a, rwkv_r_k, rwkv_lnx_g, rwkv_lnx_b, rwkv_w_o, conv_w_in, conv_w, conv_w_out, ln_mix_g, ln_mix_b, router_w, router_b, moe_w_gu, moe_b_gu, moe_w_down, moe_b_down, ln_ffn_g, ln_ffn_b, ple_w_proj, ple_w_gate, ple_b_gate, ln_ple_g, ln_ple_b):
    bsz, t, d = x.shape
    n = bsz * t
    tm_exp = 512
    assert d == SUBLANES * LANES and t % WKV_CHUNK == 0 and (n * TOP_K) % tm_exp == 0
    tm_seq = _tile(t, 256)
    tm_tok = _tile(n, 256)
    tm_big = _tile(t, 512)
    tm_cmb = _tile(n // 2, 256)
    head = jnp.arange(d, dtype=jnp.int32) // HEAD_SIZE
    bd = (head[:, None] == head[None, :]).astype(BF16)
    bf = lambda z: z.astype(BF16)
    zeros = jnp.zeros((d,), F32)
    router_wp = jnp.zeros((DEPTH, d, LANES), F32).at[:, :, :N_EXPERTS].set(router_w)
    router_bp = jnp.zeros((DEPTH, 1, LANES), F32).at[:, 0, :N_EXPERTS].set(router_b)
    p_all = p.reshape(DEPTH * n, p.shape[-1])

    v_first = None
    for i in range(DEPTH):
        j = i // 2
        if i % 2 == 0:
            has_vlora = j > 0
            vec = jnp.stack([rwkv_w0[j], rwkv_a0[j], rwkv_v0[j - 1] if has_vlora else zeros, rwkv_k_k[j],
                             rwkv_k_a[j], zeros, zeros, zeros])
            r, logw, k, v, a, b, g = _rwkv_pre(
                x, v_first, rwkv_mix[j], vec, bf(rwkv_w_rkv[j]), bf(rwkv_w1[j]), bf(rwkv_w2[j]), bf(rwkv_a1[j]),
                bf(rwkv_a2[j]), bf(rwkv_g1[j]), bf(rwkv_g2[j]),
                bf(rwkv_v1[j - 1]) if has_vlora else None, bf(rwkv_v2[j - 1]) if has_vlora else None, bd, tm_seq)
            if not has_vlora:
                v_first = v
            y = _wkv(r, logw, k, v, a, b)
            vec = jnp.stack([rwkv_r_k[j].reshape(d), rwkv_lnx_g[j], rwkv_lnx_b[j], ln_mix_g[i], ln_mix_b[i],
                             zeros, zeros, zeros])
            flat = lambda z: z.reshape(n, d)
            x2 = _rwkv_post(flat(y), flat(r), flat(k), flat(v), flat(g), flat(x), vec, bf(rwkv_w_o[j]), bd, tm_big)
        else:
            vec = jnp.stack([ln_mix_g[i], ln_mix_b[i]] + [zeros] * 6)
            x2 = _conv_mix(x, bf(conv_w_in[j]), conv_w[j], bf(conv_w_out[j]), vec, tm_big).reshape(n, d)
        route, gate, counts = _router(x2, router_wp[i], router_bp[i], _tile(n, 512))
        slots, block_e, n_active, pad_end, n_rows = _routing(route, counts, tm_exp)
        xs = _dispatch(x2, slots, pad_end, n_rows, tm_tok, tm_exp)
        y_rows = _experts(xs, block_e, n_active, i, moe_w_gu, moe_b_gu, moe_w_down, moe_b_down, tm_exp)
        vec = jnp.stack([ln_ffn_g[i], ln_ffn_b[i], ple_b_gate[i], ln_ple_g[i], ln_ple_b[i], zeros, zeros, zeros])
        x = _combine(y_rows, slots, gate, x2, i, p_all, bf(ple_w_proj[i]), bf(ple_w_gate[i]), vec,
                     tm_cmb).reshape(bsz, t, d)
    return x
```

```python
import functools

import jax
import jax.numpy as jnp
from jax import lax
from jax.experimental import pallas as pl
from jax.experimental.pallas import tpu as pltpu

DEPTH = 4
HEAD_SIZE = 64
LANES = 128
SUBLANES = 8
N_EXPERTS = 32
TOP_K = 4
GN_EPS = HEAD_SIZE * 1e-5
LN_EPS = 1e-5
SWIGLU_LIMIT = 7.0
SWIGLU_ALPHA = 1.702
DEEPNORM_ALPHA = (2.0 * DEPTH) ** 0.25
WKV_CHUNK = 64
WKV_STEP_CHUNKS = 2
VMEM_LIMIT = 56 * 1024 * 1024

F32 = jnp.float32
BF16 = jnp.bfloat16


def _dot(a, b):
    return jnp.dot(a, b, preferred_element_type=F32)


def _dot_nt(a, b):
    return lax.dot_general(a, b, (((1,), (1,)), ((), ())), preferred_element_type=F32)


def _dot_tn(a, b):
    return lax.dot_general(a, b, (((0,), (0,)), ((), ())), preferred_element_type=F32)


def _bdot(a, b):
    return _dot(a.astype(BF16), b)


def _headsum(z, bd, split=False):
    hi = z.astype(BF16)
    if not split:
        return _dot(hi, bd)
    lo = (z - hi.astype(F32)).astype(BF16)
    return _dot(hi, bd) + _dot(lo, bd)


def _layer_norm(x, g, b):
    mu = jnp.mean(x, axis=-1, keepdims=True)
    d = x - mu
    var = jnp.mean(d * d, axis=-1, keepdims=True)
    return d * lax.rsqrt(var + LN_EPS) * g + b


def _sigmoid(x):
    return 1.0 / (1.0 + jnp.exp(-x))


def _softplus(x):
    return jnp.maximum(x, 0.0) + jnp.log(1.0 + jnp.exp(-jnp.abs(x)))


def _shift_rows(x, prev_rows, n):
    out = pltpu.roll(x, n, axis=0)
    row = lax.broadcasted_iota(jnp.int32, x.shape, 0)
    p = prev_rows.shape[0]
    for i in range(n):
        out = jnp.where(row == i, prev_rows[p - n + i:p - n + i + 1, :], out)
    return out


def _params(*sem):
    return pltpu.CompilerParams(dimension_semantics=sem, vmem_limit_bytes=VMEM_LIMIT)


def _const_spec(shape):
    nd = len(shape)
    return pl.BlockSpec(shape, lambda *_: (0,) * nd)


def _to_row_tiles(ref, x):
    for s in range(SUBLANES):
        ref[pl.ds(s, x.shape[0], stride=SUBLANES), :] = x[:, s * LANES:(s + 1) * LANES]


def _from_row_tiles(ref, first_row, rows):
    return jnp.concatenate([ref[pl.ds(first_row * SUBLANES + s, rows, stride=SUBLANES), :] for s in range(SUBLANES)],
                           axis=1)


def _tile_of(ref, row):
    return ref.at[pl.ds(pl.multiple_of(row * SUBLANES, SUBLANES), SUBLANES), :]


def _rwkv_pre_kernel(*refs, has_vlora):
    if has_vlora:
        (x_ref, xp_ref, vf_ref, mix_ref, vec_ref, wrkv_ref, w1_ref, w2_ref, a1_ref, a2_ref, g1_ref, g2_ref,
         v1_ref, v2_ref, bd_ref, r_out, w_out, k_out, v_out, a_out, b_out, g_out) = refs
    else:
        (x_ref, xp_ref, mix_ref, vec_ref, wrkv_ref, w1_ref, w2_ref, a1_ref, a2_ref, g1_ref, g2_ref,
         bd_ref, r_out, w_out, k_out, v_out, a_out, b_out, g_out) = refs
    t = pl.program_id(1)
    x = x_ref[0]
    prev = jnp.where(t == 0, 0.0, xp_ref[0])
    xx = _shift_rows(x, prev, 1) - x
    mix = mix_ref[...]
    vec = vec_ref[...]
    w0, a0, v0, k_k, k_a = (vec[i:i + 1, :] for i in range(5))

    def mixed(i):
        return (x + xx * mix[i:i + 1, :]).astype(BF16)

    xv = mixed(2)
    r = _dot(mixed(0), wrkv_ref[0])
    k = _dot(mixed(1), wrkv_ref[1])
    v = _dot(xv, wrkv_ref[2])
    ww = w0 + _bdot(jnp.tanh(_dot(mixed(3), w1_ref[...])), w2_ref[...])
    logw = -jnp.exp(-_softplus(-ww) - 0.5)
    a = _sigmoid(a0 + _bdot(_dot(mixed(4), a1_ref[...]), a2_ref[...]))
    g = _bdot(_sigmoid(_dot(mixed(5), g1_ref[...])), g2_ref[...])
    if has_vlora:
        v = v + (vf_ref[0] - v) * _sigmoid(v0 + _bdot(_dot(xv, v1_ref[...]), v2_ref[...]))
    kk = k * k_k
    kk = kk / jnp.maximum(jnp.sqrt(_headsum(kk * kk, bd_ref[...])), 1e-12)
    r_out[0] = r
    w_out[0] = logw
    k_out[0] = k * (1.0 + (a - 1.0) * k_a)
    v_out[0] = v
    a_out[0] = -kk
    b_out[0] = kk * a
    g_out[0] = g


def _rwkv_pre(x, v_first, mix, vec, wrkv, w1, w2, a1, a2, g1, g2, v1, v2, bd, tm):
    bsz, t, d = x.shape
    has_vlora = v_first is not None
    tok = pl.BlockSpec((1, tm, d), lambda b, i: (b, i, 0))
    prev = pl.BlockSpec((1, 8, d), lambda b, i: (b, jnp.maximum(i * (tm // 8) - 1, 0), 0))
    ins = [x, x] + ([v_first] if has_vlora else []) + [mix, vec, wrkv, w1, w2, a1, a2, g1, g2]
    specs = [tok, prev] + ([tok] if has_vlora else []) + [_const_spec(z.shape) for z in
                                                          (mix, vec, wrkv, w1, w2, a1, a2, g1, g2)]
    if has_vlora:
        ins += [v1, v2]
        specs += [_const_spec(v1.shape), _const_spec(v2.shape)]
    ins.append(bd)
    specs.append(_const_spec(bd.shape))
    out = jax.ShapeDtypeStruct((bsz, t, d), F32)
    return pl.pallas_call(
        functools.partial(_rwkv_pre_kernel, has_vlora=has_vlora),
        grid=(bsz, t // tm),
        in_specs=specs,
        out_specs=[tok] * 7,
        out_shape=[out] * 7,
        compiler_params=_params("arbitrary", "arbitrary"),
        name="rwkv_pre",
    )(*ins)


def _wkv_kernel(r_ref, w_ref, k_ref, v_ref, a_ref, b_ref, y_ref, s_ref):
    n_pairs = s_ref.shape[0]
    L = WKV_CHUNK
    n_chunks = w_ref.shape[1] // L
    pairs = range(n_chunks * n_pairs)
    n = 2 * L

    @pl.when(pl.program_id(1) == 0)
    def _():
        s_ref[...] = jnp.zeros_like(s_ref)

    row = lax.broadcasted_iota(jnp.int32, (n, n), 0)
    col = lax.broadcasted_iota(jnp.int32, (n, n), 1)
    strict = row > col
    incl = row >= col
    eye = (row == col).astype(F32)
    tri = (lax.broadcasted_iota(jnp.int32, (L, L), 0) >= lax.broadcasted_iota(jnp.int32, (L, L), 1)).astype(BF16)
    first_head = lax.broadcasted_iota(jnp.int32, (L, LANES), 1) < HEAD_SIZE

    def stack(z):
        return jnp.concatenate([jnp.where(first_head, z, 0.0), jnp.where(first_head, 0.0, z)], axis=0)

    def lanes(ref, p):
        ci, hp = divmod(p, n_pairs)
        return ref[0, ci * L:(ci + 1) * L, hp * LANES:(hp + 1) * LANES]

    c_all = []
    for ci in range(n_chunks):
        lw = w_ref[0, ci * L:(ci + 1) * L, :]
        hi = lw.astype(BF16)
        lo = (lw - hi.astype(F32)).astype(BF16)
        c_all.append(_dot(tri, hi) + _dot(tri, lo))
    logw = [lanes(w_ref, p) for p in pairs]
    c = [c_all[p // n_pairs][:, (p % n_pairs) * LANES:(p % n_pairs + 1) * LANES] for p in pairs]
    g_inc = [jnp.exp(ci) for ci in c]
    g_inv = [jnp.exp(-ci) for ci in c]
    a2 = [stack(lanes(a_ref, p) * jnp.exp(c[p] - logw[p])) for p in pairs]
    r2 = [stack(lanes(r_ref, p) * g_inc[p]) for p in pairs]
    b2 = [stack(lanes(b_ref, p) * g_inv[p]).astype(BF16) for p in pairs]
    k2 = [stack(lanes(k_ref, p) * g_inv[p]).astype(BF16) for p in pairs]
    v2 = [stack(lanes(v_ref, p)).astype(BF16) for p in pairs]
    gram = [_dot_nt(jnp.concatenate([a2[p], r2[p]], axis=0).astype(BF16), jnp.concatenate([b2[p], k2[p]], axis=0))
            for p in pairs]
    a_ab = [jnp.where(strict, gm[:n, :n], 0.0) for gm in gram]
    a_ak = [jnp.where(strict, gm[:n, n:], 0.0) for gm in gram]
    a_rb = [jnp.where(incl, gm[n:, :n], 0.0).astype(BF16) for gm in gram]
    a_rk = [jnp.where(incl, gm[n:, n:], 0.0).astype(BF16) for gm in gram]
    x1 = [_bdot(a_ak[p], v2[p]) for p in pairs]
    inv = [eye + m for m in a_ab]
    q = a_ab
    for _ in range(L.bit_length() - 2):
        qb = [m.astype(BF16) for m in q]
        q = [_dot(m, m) for m in qb]
        inv = [inv[p] + _bdot(inv[p], q[p].astype(BF16)) for p in pairs]
    wu = [_bdot(inv[p], jnp.concatenate([a2[p], x1[p]], axis=1).astype(BF16)).astype(BF16) for p in pairs]
    zt = [_dot_tn(wu[p], b2[p]) for p in pairs]
    vk = [_dot_tn(v2[p], k2[p]) for p in pairs]
    qy = [_dot(a_rb[p], wu[p]) for p in pairs]
    yv2 = [qy[p][:, LANES:] + _dot(a_rk[p], v2[p]) for p in pairs]
    for hp in range(n_pairs):
        s = s_ref[hp]
        for ci in range(n_chunks):
            p = ci * n_pairs + hp
            q2 = r2[p] + qy[p][:, :LANES]
            q_c = q2[:L] + q2[L:]
            y_v = yv2[p][:L] + yv2[p][L:]
            sb = s.astype(BF16)
            y_ref[0, ci * L:(ci + 1) * L, hp * LANES:(hp + 1) * LANES] = _dot_nt(q_c.astype(BF16), sb) + y_v
            s = (s + _bdot(sb, zt[p][:LANES].astype(BF16)) + zt[p][LANES:] + vk[p]) * g_inc[p][L - 1:L, :]
        s_ref[hp] = s


def _wkv(r, logw, k, v, a, b):
    bsz, t, d = r.shape
    step = WKV_CHUNK * WKV_STEP_CHUNKS if t % (WKV_CHUNK * WKV_STEP_CHUNKS) == 0 else WKV_CHUNK
    blk = pl.BlockSpec((1, step, d), lambda i, c: (i, c, 0))
    return pl.pallas_call(
        _wkv_kernel,
        grid=(bsz, t // step),
        in_specs=[blk] * 6,
        out_specs=blk,
        out_shape=jax.ShapeDtypeStruct((bsz, t, d), F32),
        scratch_shapes=[pltpu.VMEM((d // LANES, LANES, LANES), F32)],
        compiler_params=_params("arbitrary", "arbitrary"),
        name="wkv_scan",
    )(r, logw, k, v, a, b)


def _rwkv_post_kernel(y_ref, r_ref, k_ref, v_ref, g_ref, x_ref, vec_ref, wo_ref, bd_ref, o_ref):
    vec = vec_ref[...]
    r_k, lnx_g, lnx_b, ln_g, ln_b = (vec[i:i + 1, :] for i in range(5))
    bd = bd_ref[...]
    y = y_ref[...]
    mu = _headsum(y, bd, split=True) * (1.0 / HEAD_SIZE)
    dy = y - mu
    var = _headsum(dy * dy, bd) * (1.0 / HEAD_SIZE)
    yn = dy * lax.rsqrt(var + GN_EPS) * lnx_g + lnx_b
    bonus = _headsum(r_ref[...] * k_ref[...] * r_k, bd) * v_ref[...]
    mixed = _bdot((yn + bonus) * g_ref[...], wo_ref[...])
    o_ref[...] = _layer_norm(DEEPNORM_ALPHA * x_ref[...] + mixed, ln_g, ln_b)


def _rwkv_post(y, r, k, v, g, x2, vec, wo, bd, tm):
    n, d = x2.shape
    tok = pl.BlockSpec((tm, d), lambda i: (i, 0))
    return pl.pallas_call(
        _rwkv_post_kernel,
        grid=(n // tm,),
        in_specs=[tok] * 6 + [_const_spec(vec.shape), _const_spec(wo.shape), _const_spec(bd.shape)],
        out_specs=tok,
        out_shape=jax.ShapeDtypeStruct((n, d), F32),
        compiler_params=_params("arbitrary"),
        name="rwkv_post",
    )(y, r, k, v, g, x2, vec, wo, bd)


def _conv_kernel(x_ref, xp_ref, win_ref, cw_ref, wout_ref, vec_ref, o_ref):
    d = x_ref.shape[-1]
    t = pl.program_id(1)
    x = x_ref[0]
    proj = _bdot(x, win_ref[...])
    gate_b = proj[:, :d]
    ch = proj[:, d:2 * d] * proj[:, 2 * d:]
    pproj = _bdot(xp_ref[0], win_ref[:, d:])
    ch_prev = jnp.where(t == 0, 0.0, pproj[:, :d] * pproj[:, d:])
    cw = cw_ref[...]
    u = cw[0:1, :] * _shift_rows(ch, ch_prev, 2) + cw[1:2, :] * _shift_rows(ch, ch_prev, 1) + cw[2:3, :] * ch
    mixed = _bdot(gate_b * u, wout_ref[...])
    vec = vec_ref[...]
    o_ref[0] = _layer_norm(DEEPNORM_ALPHA * x + mixed, vec[0:1, :], vec[1:2, :])


def _conv_mix(x, win, cw, wout, vec, tm):
    bsz, t, d = x.shape
    tok = pl.BlockSpec((1, tm, d), lambda b, i: (b, i, 0))
    prev = pl.BlockSpec((1, 8, d), lambda b, i: (b, jnp.maximum(i * (tm // 8) - 1, 0), 0))
    return pl.pallas_call(
        _conv_kernel,
        grid=(bsz, t // tm),
        in_specs=[tok, prev] + [_const_spec(z.shape) for z in (win, cw, wout, vec)],
        out_specs=tok,
        out_shape=jax.ShapeDtypeStruct((bsz, t, d), F32),
        compiler_params=_params("arbitrary", "arbitrary"),
        name="conv_mix",
    )(x, x, win, cw, wout, vec)


def _router_kernel(x_ref, w_ref, b_ref, route_ref, gate_ref, cnt_ref, base_ref):
    @pl.when(pl.program_id(0) == 0)
    def _():
        base_ref[...] = jnp.zeros_like(base_ref)

    x = x_ref[...]
    x_hi = x.astype(BF16)
    x_lo = (x - x_hi.astype(F32)).astype(BF16)
    w = w_ref[...]
    w_hi = w.astype(BF16)
    w_lo = (w - w_hi.astype(F32)).astype(BF16)
    logits = _dot(x_hi, w_hi) + _dot(x_hi, w_lo) + _dot(x_lo, w_hi) + b_ref[...]
    tm = logits.shape[0]
    cur = jnp.transpose(logits)[:N_EXPERTS, :]
    expert = lax.broadcasted_iota(jnp.int32, cur.shape, 0)
    vals, idxs, chosen = [], [], []
    for _ in range(TOP_K):
        m = jnp.max(cur, axis=0, keepdims=True)
        sel = jnp.min(jnp.where(cur == m, expert, N_EXPERTS), axis=0, keepdims=True)
        ch = expert == sel
        vals.append(m)
        idxs.append(sel)
        chosen.append(ch)
        cur = jnp.where(ch, -jnp.inf, cur)
    es = [jnp.exp(m - vals[0]) for m in vals]
    tot = es[0] + es[1] + es[2] + es[3]
    onehot = chosen[0].astype(F32)
    for ch in chosen[1:]:
        onehot = onehot + ch.astype(F32)
    earlier = lax.broadcasted_iota(jnp.int32, (tm, tm), 0) < lax.broadcasted_iota(jnp.int32, (tm, tm), 1)
    pos_all = base_ref[:, 0:1] + _dot(onehot.astype(BF16), earlier.astype(BF16))
    pos = [jnp.sum(jnp.where(ch, pos_all, 0.0), axis=0, keepdims=True).astype(jnp.int32) for ch in chosen]
    route_ref[0] = jnp.concatenate(idxs + pos, axis=0)
    gate_ref[0] = jnp.concatenate([e / tot for e in es] + [jnp.zeros_like(tot)] * (SUBLANES - TOP_K), axis=0)
    total = base_ref[...] + jnp.sum(onehot, axis=1, keepdims=True)
    base_ref[...] = total
    cnt_ref[...] = total


def _router(x2, w, b, tm):
    n, d = x2.shape
    n_tiles = n // tm
    tok = pl.BlockSpec((tm, d), lambda i: (i, 0))
    out = pl.BlockSpec((1, SUBLANES, tm), lambda i: (i, 0, 0))
    cnt = pl.BlockSpec((N_EXPERTS, LANES), lambda i: (0, 0))
    route_t, gate_t, counts = pl.pallas_call(
        _router_kernel,
        grid=(n_tiles,),
        in_specs=[tok, _const_spec(w.shape), _const_spec(b.shape)],
        out_specs=[out, out, cnt],
        out_shape=[jax.ShapeDtypeStruct((n_tiles, SUBLANES, tm), jnp.int32),
                   jax.ShapeDtypeStruct((n_tiles, SUBLANES, tm), F32),
                   jax.ShapeDtypeStruct((N_EXPERTS, LANES), F32)],
        scratch_shapes=[pltpu.VMEM((N_EXPERTS, LANES), F32)],
        compiler_params=_params("arbitrary"),
        name="router",
    )(x2, w, b)
    route = route_t.transpose(0, 2, 1).reshape(n, SUBLANES)
    gate = jnp.pad(gate_t.transpose(0, 2, 1).reshape(n, SUBLANES), ((0, 0), (0, LANES - SUBLANES)))
    return route, gate, counts[:, 0]


def _dispatch_kernel(pe_ref, slot_ref, x_ref, xs_hbm, stage0, stage1, zero_ref, sem, zsem, *, tile_rows, n_steps):
    tm = x_ref.shape[0]
    i = pl.program_id(0)

    def zero_copy(e):
        first = pl.multiple_of((pe_ref[e] - tile_rows) * SUBLANES, SUBLANES)
        return pltpu.make_async_copy(zero_ref, xs_hbm.at[pl.ds(first, tile_rows * SUBLANES), :], zsem)

    @pl.when(i == 0)
    def _():
        zero_ref[...] = jnp.zeros_like(zero_ref)
        for e in range(N_EXPERTS):
            @pl.when(pe_ref[e] > 0)
            def _():
                zero_copy(e).start()
        for e in range(N_EXPERTS):
            @pl.when(pe_ref[e] > 0)
            def _():
                zero_copy(e).wait()

    def row_copy(stage, parity, j, slot):
        return pltpu.make_async_copy(_tile_of(stage, j), _tile_of(xs_hbm, slot), sem.at[parity])

    def wait_all(stage, parity):
        def body(grp, carry):
            for _ in range(SUBLANES * TOP_K):
                row_copy(stage, parity, 0, 0).wait()
            return carry

        lax.fori_loop(0, tm // SUBLANES, body, 0)

    def step(stage, parity):
        @pl.when(i >= 2)
        def _():
            wait_all(stage, parity)

        _to_row_tiles(stage, x_ref[...])

        def body(grp, carry):
            for u in range(SUBLANES):
                for k in range(TOP_K):
                    j = grp * SUBLANES + u
                    row_copy(stage, parity, j, slot_ref[0, 0, j * TOP_K + k]).start(priority=k % 2)
            return carry

        lax.fori_loop(0, tm // SUBLANES, body, 0)

    @pl.when(i % 2 == 0)
    def _():
        step(stage0, 0)

    @pl.when(i % 2 == 1)
    def _():
        step(stage1, 1)

    @pl.when(i == n_steps - 1)
    def _():
        wait_all(stage0, 0)
        if n_steps >= 2:
            wait_all(stage1, 1)


def _dispatch(x2, slots, pad_end, n_rows, tm, tm_exp):
    n, d = x2.shape
    n_tiles = n // tm
    grid_spec = pltpu.PrefetchScalarGridSpec(
        num_scalar_prefetch=1,
        grid=(n_tiles,),
        in_specs=[
            pl.BlockSpec((1, 1, TOP_K * tm), lambda i, pe: (i, 0, 0), memory_space=pltpu.SMEM),
            pl.BlockSpec((tm, d), lambda i, pe: (i, 0)),
        ],
        out_specs=pl.BlockSpec(memory_space=pl.ANY),
        scratch_shapes=[pltpu.VMEM((tm * SUBLANES, LANES), F32), pltpu.VMEM((tm * SUBLANES, LANES), F32),
                        pltpu.VMEM((tm_exp * SUBLANES, LANES), F32), pltpu.SemaphoreType.DMA((2,)),
                        pltpu.SemaphoreType.DMA(())],
    )
    return pl.pallas_call(
        functools.partial(_dispatch_kernel, tile_rows=tm_exp, n_steps=n_tiles),
        grid_spec=grid_spec,
        out_shape=jax.ShapeDtypeStruct((n_rows * SUBLANES, LANES), F32),
        compiler_params=_params("arbitrary"),
        name="dispatch",
    )(pad_end, slots.reshape(n_tiles, 1, TOP_K * tm), x2)


def _expert_kernel(be_ref, nact_ref, x_ref, wgu_ref, bgu_ref, wd_ref, bd_ref, o_ref, wgu_bf, wd_bf):
    f = wd_ref.shape[1]
    tm = x_ref.shape[0] // SUBLANES
    i = pl.program_id(0)
    active = i < nact_ref[0]
    new_expert = jnp.logical_or(i == 0, be_ref[i] != be_ref[jnp.maximum(i - 1, 0)])

    @pl.when(jnp.logical_and(active, new_expert))
    def _():
        wgu_bf[...] = wgu_ref[0].astype(BF16)
        wd_bf[...] = wd_ref[0].astype(BF16)

    @pl.when(active)
    def _():
        hcat = _bdot(_from_row_tiles(x_ref, 0, tm), wgu_bf[...]) + bgu_ref[0]
        glu = jnp.minimum(hcat[:, :f], SWIGLU_LIMIT)
        lin = jnp.clip(hcat[:, f:], -SWIGLU_LIMIT, SWIGLU_LIMIT)
        act = glu * _sigmoid(SWIGLU_ALPHA * glu) * (lin + 1.0)
        _to_row_tiles(o_ref, _bdot(act, wd_bf[...]) + bd_ref[0])

    @pl.when(jnp.logical_not(active))
    def _():
        o_ref[...] = jnp.zeros_like(o_ref)


def _experts(xs, block_e, n_active, layer, wgu, bgu, wd, bdn, tm):
    d = wgu.shape[2]
    n_blocks = xs.shape[0] // (tm * SUBLANES)
    f2 = wgu.shape[-1]
    f = wd.shape[2]
    nl = wgu.shape[0]
    wgu, wd = wgu.reshape(nl * N_EXPERTS, d, f2), wd.reshape(nl * N_EXPERTS, f, d)
    bgu, bdn = bgu.reshape(nl * N_EXPERTS, 1, f2), bdn.reshape(nl * N_EXPERTS, 1, d)
    first = layer * N_EXPERTS
    grid_spec = pltpu.PrefetchScalarGridSpec(
        num_scalar_prefetch=2,
        grid=(n_blocks,),
        in_specs=[
            pl.BlockSpec((tm * SUBLANES, LANES), lambda i, be, na: (jnp.minimum(i, na[0] - 1), 0)),
            pl.BlockSpec((1, d, f2), lambda i, be, na: (first + be[i], 0, 0)),
            pl.BlockSpec((1, 1, f2), lambda i, be, na: (first + be[i], 0, 0)),
            pl.BlockSpec((1, f, d), lambda i, be, na: (first + be[i], 0, 0)),
            pl.BlockSpec((1, 1, d), lambda i, be, na: (first + be[i], 0, 0)),
        ],
        out_specs=pl.BlockSpec((tm * SUBLANES, LANES), lambda i, be, na: (i, 0)),
        scratch_shapes=[pltpu.VMEM((d, f2), BF16), pltpu.VMEM((f, d), BF16)],
    )
    return pl.pallas_call(
        _expert_kernel,
        grid_spec=grid_spec,
        out_shape=jax.ShapeDtypeStruct(xs.shape, F32),
        compiler_params=_params("arbitrary"),
        name="experts",
    )(block_e, n_active, xs, wgu, bgu, wd, bdn)


def _combine_kernel(idx_ref, idxn_ref, y_hbm, gate_ref, x_ref, p_ref, wproj_ref, wgate_ref, vec_ref, o_ref,
                    buf_a, buf_b, sem, *, n_steps):
    tm = x_ref.shape[0] // 2
    i = pl.program_id(0)
    vec = vec_ref[...]
    ffn_g, ffn_b, b_gate, ple_g, ple_b = (vec[j:j + 1, :] for j in range(5))

    def row_copy(src_row, buf, which, j):
        return pltpu.make_async_copy(_tile_of(y_hbm, src_row), _tile_of(buf, j), sem.at[which])

    def start_rows(ids_ref, first, buf, which):
        for j in range(tm):
            for k in range(TOP_K):
                row_copy(ids_ref[0, 0, (first + j) * TOP_K + k], buf, which, k * tm + j).start(priority=k % 2)

    def wait_rows(buf, which):
        def body(grp, carry):
            for _ in range(SUBLANES * TOP_K):
                row_copy(0, buf, which, 0).wait()
            return carry

        lax.fori_loop(0, tm // SUBLANES, body, 0)

    def compute(buf, first):
        tok = pl.ds(first, tm)
        gate = gate_ref[tok, :]
        ffn = None
        for k in range(TOP_K):
            part = _from_row_tiles(buf, k * tm, tm) * gate[:, k:k + 1]
            ffn = part if ffn is None else ffn + part
        x = _layer_norm(DEEPNORM_ALPHA * x_ref[tok, :] + ffn, ffn_g, ffn_b)
        ple = _bdot(p_ref[tok, :], wproj_ref[...]) * _sigmoid(_bdot(x, wgate_ref[...]) + b_gate)
        o_ref[tok, :] = _layer_norm(DEEPNORM_ALPHA * x + ple, ple_g, ple_b)

    @pl.when(i == 0)
    def _():
        start_rows(idx_ref, 0, buf_a, 0)

    wait_rows(buf_a, 0)
    start_rows(idx_ref, tm, buf_b, 1)
    compute(buf_a, 0)
    wait_rows(buf_b, 1)
    start_rows(idxn_ref, 0, buf_a, 0)
    compute(buf_b, tm)

    @pl.when(i == n_steps - 1)
    def _():
        wait_rows(buf_a, 0)


def _combine(y_rows, slots, gate, x2, layer, p2, wproj, wgate, vec, tm):
    n, d = x2.shape
    n_steps = n // (2 * tm)
    p_first = layer * n_steps
    idx = slots.reshape(n_steps, 1, 2 * TOP_K * tm)
    tok = pl.BlockSpec((2 * tm, d), lambda i: (i, 0))
    buf = pltpu.VMEM((TOP_K * tm * SUBLANES, LANES), F32)
    return pl.pallas_call(
        functools.partial(_combine_kernel, n_steps=n_steps),
        grid=(n_steps,),
        in_specs=[
            pl.BlockSpec((1, 1, 2 * TOP_K * tm), lambda i: (i, 0, 0), memory_space=pltpu.SMEM),
            pl.BlockSpec((1, 1, 2 * TOP_K * tm), lambda i: (jnp.minimum(i + 1, n_steps - 1), 0, 0),
                         memory_space=pltpu.SMEM),
            pl.BlockSpec(memory_space=pl.ANY),
            pl.BlockSpec((2 * tm, LANES), lambda i: (i, 0)),
            tok,
            pl.BlockSpec((2 * tm, p2.shape[1]), lambda i: (p_first + i, 0)),
            _const_spec(wproj.shape),
            _const_spec(wgate.shape),
            _const_spec(vec.shape),
        ],
        out_specs=tok,
        out_shape=jax.ShapeDtypeStruct((n, d), F32),
        scratch_shapes=[buf, buf, pltpu.SemaphoreType.DMA((2,))],
        compiler_params=_params("arbitrary"),
        name="combine_ple",
    )(idx, idx, y_rows, gate, x2, p2, wproj, wgate, vec)


def _routing(route, counts, tm):
    n = route.shape[0]
    experts = route[:, :TOP_K]
    pos = route[:, TOP_K:2 * TOP_K]
    counts = counts.astype(jnp.int32)
    padded = (counts + tm - 1) // tm * tm
    pad_end = jnp.cumsum(padded).astype(jnp.int32)
    pad_start = pad_end - padded
    start_of = jnp.sum(jnp.where(experts[:, :, None] == jnp.arange(N_EXPERTS, dtype=jnp.int32), pad_start, 0), axis=-1)
    slots = (pos + start_of).astype(jnp.int32)
    n_blocks = n * TOP_K // tm + N_EXPERTS
    block_start = jnp.arange(n_blocks, dtype=jnp.int32) * tm
    block_e = jnp.minimum(jnp.sum(pad_end[None, :] <= block_start[:, None], axis=1), N_EXPERTS - 1).astype(jnp.int32)
    n_active = (pad_end[-1:] // tm).astype(jnp.int32)
    return slots, block_e, n_active, jnp.where(padded > 0, pad_end, 0), n_blocks * tm


def _tile(n, want):
    t = min(n, want)
    assert n % t == 0 and t % 8 == 0, (n, t)
    return t


def kernel(x, p, rwkv_mix, rwkv_w_rkv, rwkv_w0, rwkv_w1, rwkv_w2, rwkv_a0, rwkv_a1, rwkv_a2, rwkv_v0, rwkv_v1, rwkv_v2, rwkv_g1, rwkv_g2, rwkv_k_k, rwkv_k_a, rwkv_r_k, rwkv_lnx_g, rwkv_lnx_b, rwkv_w_o, conv_w_in, conv_w, conv_w_out, ln_mix_g, ln_mix_b, router_w, router_b, moe_w_gu, moe_b_gu, moe_w_down, moe_b_down, ln_ffn_g, ln_ffn_b, ple_w_proj, ple_w_gate, ple_b_gate, ln_ple_g, ln_ple_b):
    bsz, t, d = x.shape
    n = bsz * t
    tm_exp = 512
    assert d == SUBLANES * LANES and t % WKV_CHUNK == 0 and (n * TOP_K) % tm_exp == 0
    tm_big = _tile(t, 512)
    tm_cmb = _tile(n // 2, 256)
    head = jnp.arange(d, dtype=jnp.int32) // HEAD_SIZE
    bd = (head[:, None] == head[None, :]).astype(BF16)
    bf = lambda z: z.astype(BF16)
    zeros = jnp.zeros((d,), F32)
    router_wp = jnp.zeros((DEPTH, d, LANES), F32).at[:, :, :N_EXPERTS].set(router_w)
    router_bp = jnp.zeros((DEPTH, 1, LANES), F32).at[:, 0, :N_EXPERTS].set(router_b)
    p_all = p.reshape(DEPTH * n, p.shape[-1])

    v_first = None
    for i in range(DEPTH):
        j = i // 2
        if i % 2 == 0:
            has_vlora = j > 0
            vec = jnp.stack([rwkv_w0[j], rwkv_a0[j], rwkv_v0[j - 1] if has_vlora else zeros, rwkv_k_k[j],
                             rwkv_k_a[j], zeros, zeros, zeros])
            r, logw, k, v, a, b, g = _rwkv_pre(
                x, v_first, rwkv_mix[j], vec, bf(rwkv_w_rkv[j]), bf(rwkv_w1[j]), bf(rwkv_w2[j]), bf(rwkv_a1[j]),
                bf(rwkv_a2[j]), bf(rwkv_g1[j]), bf(rwkv_g2[j]),
                bf(rwkv_v1[j - 1]) if has_vlora else None, bf(rwkv_v2[j - 1]) if has_vlora else None, bd, tm_big)
            if not has_vlora:
                v_first = v
            y = _wkv(r, logw, k, v, a, b)
            vec = jnp.stack([rwkv_r_k[j].reshape(d), rwkv_lnx_g[j], rwkv_lnx_b[j], ln_mix_g[i], ln_mix_b[i],
                             zeros, zeros, zeros])
            flat = lambda z: z.reshape(n, d)
            x2 = _rwkv_post(flat(y), flat(r), flat(k), flat(v), flat(g), flat(x), vec, bf(rwkv_w_o[j]), bd, tm_big)
        else:
            vec = jnp.stack([ln_mix_g[i], ln_mix_b[i]] + [zeros] * 6)
            x2 = _conv_mix(x, bf(conv_w_in[j]), conv_w[j], bf(conv_w_out[j]), vec, tm_big).reshape(n, d)
        route, gate, counts = _router(x2, router_wp[i], router_bp[i], _tile(n, 512))
        slots, block_e, n_active, pad_end, n_rows = _routing(route, counts, tm_exp)
        xs = _dispatch(x2, slots, pad_end, n_rows, _tile(n, 512), tm_exp)
        y_rows = _experts(xs, block_e, n_active, i, moe_w_gu, moe_b_gu, moe_w_down, moe_b_down, tm_exp)
        vec = jnp.stack([ln_ffn_g[i], ln_ffn_b[i], ple_b_gate[i], ln_ple_g[i], ln_ple_b[i], zeros, zeros, zeros])
        x = _combine(y_rows, slots, gate, x2, i, p_all, bf(ple_w_proj[i]), bf(ple_w_gate[i]), vec,
                     tm_cmb).reshape(bsz, t, d)
    return x
```
